```python
import jax, jax.numpy as jnp
from jax import lax
import numpy as np

D_MODEL = 1024
BATCH = 16
SEQ = 4096
DEPTH = 4

MIX_WIDTH = D_MODEL
A_HEADS = 8
A_HEAD_DIM = 64
IDX_HEADS = 8
IDX_DIM = 64
TOPK_MAX = 256
Q_BLOCK = 128
B_HEADS = 4
B_KEY_DIM = 64
B_VAL_DIM = 128
GATE_RANK = 16
GATE_TAU = 16.0
CHUNK = 64
D_FF = 2816
N_EXPERTS = 8
TOP_K_EXPERTS = 2
MOE_BLOCK = 256
N_DENSE = (DEPTH + 1) // 2
N_MOE = DEPTH // 2
DN_ALPHA = (2 * DEPTH) ** 0.25
DN_BETA = (8 * DEPTH) ** -0.25
NORM_EPS = 1e-5

A_Q_W = A_HEADS * A_HEAD_DIM
A_K_W = A_HEAD_DIM
A_V_W = A_HEAD_DIM
I_Q_W = IDX_HEADS * IDX_DIM
I_K_W = IDX_DIM
I_W_W = IDX_HEADS
B_Q_W = B_HEADS * B_KEY_DIM
B_K_W = B_HEADS * B_KEY_DIM
B_V_W = B_HEADS * B_VAL_DIM
B_G_W = B_HEADS * B_VAL_DIM
B_A_W = GATE_RANK
SEG_WIDTHS = (A_Q_W, A_K_W, A_V_W, I_Q_W, I_K_W, I_W_W, B_Q_W, B_K_W, B_V_W, B_G_W, B_A_W)
VALUE_SEGS = (2, 8)
PROJ_WIDTH = A_Q_W + A_K_W + A_V_W + I_Q_W + I_K_W + I_W_W + B_Q_W + B_K_W + B_V_W + B_G_W + B_A_W

kernel_name = 'hybrid_dsa_gla_deepnorm_moe'


def _split_points():
    pts, acc = [], 0
    for w in SEG_WIDTHS[:-1]:
        acc += w
        pts.append(acc)
    return pts


def layer_norm(x, g, b):
    xf = x.astype(jnp.float32)
    mu = jnp.mean(xf, axis=-1, keepdims=True)
    var = jnp.mean(jnp.square(xf - mu), axis=-1, keepdims=True)
    y = (xf - mu) * lax.rsqrt(var + NORM_EPS) * g.astype(jnp.float32) + b.astype(jnp.float32)
    return y.astype(x.dtype)


def rms_norm(x, g):
    xf = x.astype(jnp.float32)
    y = xf * lax.rsqrt(jnp.mean(jnp.square(xf), axis=-1, keepdims=True) + NORM_EPS)
    return (y * g.astype(jnp.float32)).astype(x.dtype)


def alibi_slopes(n):
    return jnp.exp2(-8.0 * jnp.arange(1, n + 1, dtype=jnp.float32) / n)


def dsa_attention(q, k, v, q_idx, k_idx, w_idx):
    bsz, L = q.shape[0], q.shape[1]
    k_top = min(TOPK_MAX, L // 4)
    n_blocks = L // Q_BLOCK
    slopes = alibi_slopes(A_HEADS)
    key_pos = jnp.arange(L)
    scale = A_HEAD_DIM ** -0.5
    idx_scale = IDX_DIM ** -0.5
    k_idx_f = k_idx.astype(jnp.float32)
    gather = jax.vmap(lambda table, ids: table[ids])

    def block(i):
        t0 = i * Q_BLOCK
        qb = lax.dynamic_slice_in_dim(q, t0, Q_BLOCK, axis=1)
        qib = lax.dynamic_slice_in_dim(q_idx, t0, Q_BLOCK, axis=1)
        wib = lax.dynamic_slice_in_dim(w_idx, t0, Q_BLOCK, axis=1)
        qpos = t0 + jnp.arange(Q_BLOCK)
        s_idx = jnp.einsum('bthd,bsd->bths', qib.astype(jnp.float32), k_idx_f) * idx_scale
        scores = jnp.einsum('bth,bths->bts', wib.astype(jnp.float32), jax.nn.relu(s_idx))
        causal = key_pos[None, :] <= qpos[:, None]
        scores = jnp.where(causal[None], scores, -jnp.inf)
        _, sel = lax.top_k(scores, k_top)
        k_sel = gather(k, sel)
        v_sel = gather(v, sel)
        dist = (qpos[None, :, None] - sel).astype(jnp.float32)
        logits = jnp.einsum('bthd,btkd->bthk', qb, k_sel).astype(jnp.float32) * scale
        logits = logits - slopes[None, None, :, None] * dist[:, :, None, :]
        logits = jnp.where((dist >= 0)[:, :, None, :], logits, -jnp.inf)
        p = jax.nn.softmax(logits, axis=-1)
        return jnp.einsum('bthk,btkd->bthd', p.astype(v.dtype), v_sel)

    outs = lax.map(block, jnp.arange(n_blocks))
    return outs.transpose(1, 0, 2, 3, 4).reshape(bsz, L, A_HEADS * A_HEAD_DIM)


def gla(q, k, v, log_a):
    bsz, L = q.shape[0], q.shape[1]
    nc = L // CHUNK

    def to_chunks(z):
        return z.astype(jnp.float32).reshape(bsz, nc, CHUNK, B_HEADS, -1).transpose(1, 0, 3, 2, 4)

    qc = to_chunks(q) * (B_KEY_DIM ** -0.5)
    kc, vc, gc = to_chunks(k), to_chunks(v), to_chunks(log_a)
    tri = jnp.tril(jnp.ones((CHUNK, CHUNK), dtype=bool))

    def step(state, inp):
        qi, ki, vi, gi = inp
        b = jnp.cumsum(gi, axis=2)
        b_last = b[:, :, -1:, :]
        inter = jnp.einsum('bhtd,bhde->bhte', qi * jnp.exp(b), state)
        diff = b[:, :, :, None, :] - b[:, :, None, :, :]
        decay = jnp.exp(jnp.where(tri[None, None, :, :, None], diff, -jnp.inf))
        attn = jnp.einsum('bhtd,bhsd,bhtsd->bhts', qi, ki, decay)
        out = inter + jnp.einsum('bhts,bhse->bhte', attn, vi)
        new_state = jnp.exp(b_last)[:, :, 0, :, None] * state + \
            jnp.einsum('bhsd,bhse->bhde', ki * jnp.exp(b_last - b), vi)
        return new_state, out

    init = jnp.zeros((bsz, B_HEADS, B_KEY_DIM, B_VAL_DIM), jnp.float32)
    _, outs = lax.scan(step, init, (qc, kc, vc, gc))
    return outs.transpose(1, 0, 3, 2, 4).reshape(bsz, L, B_HEADS, B_VAL_DIM).astype(q.dtype)


def hybrid_mixer(h, w_in, w_a2, b_a, gla_g, w_out):
    bsz, L, _ = h.shape
    proj = h @ w_in
    aq, ak, av, iq, ik, iw, bq, bk, bv, bg, ba = jnp.split(proj, _split_points(), axis=-1)
    y_a = dsa_attention(
        aq.reshape(bsz, L, A_HEADS, A_HEAD_DIM), ak, av,
        iq.reshape(bsz, L, IDX_HEADS, IDX_DIM), ik, iw * (IDX_HEADS ** -0.5))
    log_a = jax.nn.log_sigmoid((ba @ w_a2 + b_a).astype(jnp.float32)) / GATE_TAU
    o = gla(bq.reshape(bsz, L, B_HEADS, B_KEY_DIM), bk.reshape(bsz, L, B_HEADS, B_KEY_DIM),
            bv.reshape(bsz, L, B_HEADS, B_VAL_DIM), log_a.reshape(bsz, L, B_HEADS, B_KEY_DIM))
    o = rms_norm(o, gla_g) * jax.nn.silu(bg.reshape(bsz, L, B_HEADS, B_VAL_DIM))
    y = jnp.concatenate([y_a, o.reshape(bsz, L, B_HEADS * B_VAL_DIM)], axis=-1)
    return y @ w_out


def swiglu(h, w_gate, w_up, w_down):
    return (jax.nn.silu(h @ w_gate) * (h @ w_up)) @ w_down


def moe_swiglu(h, w_router, b_router, we_gate, we_up, we_down):
    bsz, L, d = h.shape
    n_tok = bsz * L
    xf = h.reshape(n_tok, d)
    logits = (xf @ w_router + b_router).astype(jnp.float32)
    top_val, top_idx = lax.top_k(logits, TOP_K_EXPERTS)
    gates = jax.nn.softmax(top_val, axis=-1)
    n_assign = n_tok * TOP_K_EXPERTS
    e_flat = top_idx.reshape(-1)
    tok_flat = jnp.repeat(jnp.arange(n_tok, dtype=jnp.int32), TOP_K_EXPERTS)
    g_flat = gates.reshape(-1)
    order = jnp.argsort(e_flat)
    e_sorted = e_flat[order]
    counts = jnp.bincount(e_flat, length=N_EXPERTS)
    padded = ((counts + MOE_BLOCK - 1) // MOE_BLOCK) * MOE_BLOCK
    start = jnp.cumsum(counts) - counts
    pstart = jnp.cumsum(padded) - padded
    pend = pstart + padded
    dest = pstart[e_sorted] + (jnp.arange(n_assign) - start[e_sorted])
    n_blocks = -(-n_assign // MOE_BLOCK) + N_EXPERTS
    n_rows = n_blocks * MOE_BLOCK
    buf_tok = jnp.zeros((n_rows,), jnp.int32).at[dest].set(tok_flat[order])
    buf_gate = jnp.zeros((n_rows,), jnp.float32).at[dest].set(g_flat[order])
    blk_expert = jnp.minimum(
        jnp.searchsorted(pend, jnp.arange(n_blocks) * MOE_BLOCK, side='right'), N_EXPERTS - 1)

    def run_block(args):
        toks, gts, e = args
        y = swiglu(xf[toks], we_gate[e], we_up[e], we_down[e])
        return y * gts[:, None].astype(y.dtype)

    yb = lax.map(run_block, (buf_tok.reshape(n_blocks, MOE_BLOCK),
                             buf_gate.reshape(n_blocks, MOE_BLOCK), blk_expert))
    out = jax.ops.segment_sum(yb.reshape(n_rows, d), buf_tok, num_segments=n_tok)
    return out.reshape(bsz, L, d)


def setup_inputs(seed: int = 0) -> dict:
    key = jax.random.key(seed)
    ks = jax.random.split(key, 20)
    f32 = jnp.float32
    nrm = lambda k, shape: jax.random.normal(k, shape, f32)
    col_scale = jnp.concatenate([
        jnp.full((w,), DN_BETA if i in VALUE_SEGS else 1.0, f32) for i, w in enumerate(SEG_WIDTHS)])
    return {
        'x': nrm(ks[0], (BATCH, SEQ, D_MODEL)),
        'c': nrm(ks[1], (BATCH, D_MODEL)),
        'w_in': nrm(ks[2], (DEPTH, D_MODEL, PROJ_WIDTH)) * (D_MODEL ** -0.5) * col_scale,
        'w_a2': nrm(ks[3], (DEPTH, GATE_RANK, B_HEADS * B_KEY_DIM)) * (GATE_RANK ** -0.5),
        'b_a': 0.1 * nrm(ks[4], (DEPTH, B_HEADS * B_KEY_DIM)),
        'gla_norm_g': 1.0 + 0.1 * nrm(ks[5], (DEPTH, B_VAL_DIM)),
        'w_out': nrm(ks[6], (DEPTH, MIX_WIDTH, D_MODEL)) * (MIX_WIDTH ** -0.5) * DN_BETA,
        'w_mod': nrm(ks[7], (DEPTH, D_MODEL, 6 * D_MODEL)) * (D_MODEL ** -0.5) * 0.1,
        'b_mod': 0.01 * nrm(ks[8], (DEPTH, 6 * D_MODEL)),
        'ln_g': 1.0 + 0.1 * nrm(ks[9], (DEPTH, 2, D_MODEL)),
        'ln_b': 0.02 * nrm(ks[10], (DEPTH, 2, D_MODEL)),
        'ffn_w_gate': nrm(ks[11], (N_DENSE, D_MODEL, D_FF)) * (D_MODEL ** -0.5),
        'ffn_w_up': nrm(ks[12], (N_DENSE, D_MODEL, D_FF)) * (D_MODEL ** -0.5),
        'ffn_w_down': nrm(ks[13], (N_DENSE, D_FF, D_MODEL)) * (D_FF ** -0.5) * DN_BETA,
        'w_router': nrm(ks[14], (N_MOE, D_MODEL, N_EXPERTS)) * (D_MODEL ** -0.5),
        'b_router': 0.01 * nrm(ks[15], (N_MOE, N_EXPERTS)),
        'moe_w_gate': nrm(ks[16], (N_MOE, N_EXPERTS, D_MODEL, D_FF)) * (D_MODEL ** -0.5),
        'moe_w_up': nrm(ks[17], (N_MOE, N_EXPERTS, D_MODEL, D_FF)) * (D_MODEL ** -0.5),
        'moe_w_down': nrm(ks[18], (N_MOE, N_EXPERTS, D_FF, D_MODEL)) * (D_FF ** -0.5) * DN_BETA,
    }


def reference(x, c, w_in, w_a2, b_a, gla_norm_g, w_out, w_mod, b_mod, ln_g, ln_b,
              ffn_w_gate, ffn_w_up, ffn_w_down, w_router, b_router,
              moe_w_gate, moe_w_up, moe_w_down):
    c_act = jax.nn.silu(c)
    for l in range(DEPTH):
        mod = (c_act @ w_mod[l] + b_mod[l])[:, None, :]
        sh1, sc1, g1, sh2, sc2, g2 = jnp.split(mod, 6, axis=-1)
        h = x * (1.0 + sc1) + sh1
        y = hybrid_mixer(h, w_in[l], w_a2[l], b_a[l], gla_norm_g[l], w_out[l])
        x = layer_norm(DN_ALPHA * x + (1.0 + g1) * y, ln_g[l, 0], ln_b[l, 0])
        h = x * (1.0 + sc2) + sh2
        if l % 2 == 0:
            j = l // 2
            y = swiglu(h, ffn_w_gate[j], ffn_w_up[j], ffn_w_down[j])
        else:
            j = l // 2
            y = moe_swiglu(h, w_router[j], b_router[j], moe_w_gate[j], moe_w_up[j], moe_w_down[j])
        x = layer_norm(DN_ALPHA * x + (1.0 + g2) * y, ln_g[l, 1], ln_b[l, 1])
    return x
```

```python
import functools

import jax
import jax.numpy as jnp
from jax import lax
from jax.experimental import pallas as pl
from jax.experimental.pallas import tpu as pltpu

F32 = jnp.float32
BF16 = jnp.bfloat16
I32 = jnp.int32
HIGHEST = lax.Precision.HIGHEST

A_HEADS = 8
A_HEAD_DIM = 64
IDX_HEADS = 8
IDX_DIM = 64
TOPK_MAX = 256
B_HEADS = 4
B_KEY_DIM = 64
B_VAL_DIM = 128
GATE_RANK = 16
GATE_TAU = 16.0
GLA_CHUNK = 64
N_EXPERTS = 8
NORM_EPS = 1e-5

N1, N2, N3 = 1280, 1024, 640
IW_OFF = 0
BA_OFF = 8

VMEM_LIMIT = 56 * 1024 * 1024
INT_MIN = -2147483648
NEG_BIG = -1e30


def _sigmoid(x):
    return 1.0 / (1.0 + jnp.exp(-x))


def _cparams(sem):
    return pltpu.CompilerParams(dimension_semantics=sem, vmem_limit_bytes=VMEM_LIMIT)


def _mod_kernel(c_ref, w_ref, b_ref, o_ref):
    c = c_ref[...]
    ca = c * _sigmoid(c)
    o_ref[0] = jnp.dot(ca, w_ref[0], precision=HIGHEST, preferred_element_type=F32) + b_ref[0]


def _modulation(c, w_mod, b_mod):
    depth, d, n = w_mod.shape
    bsz = c.shape[0]
    tn = 1536 if n % 1536 == 0 else n
    return pl.pallas_call(
        _mod_kernel,
        grid=(depth, n // tn),
        in_specs=[
            pl.BlockSpec((bsz, d), lambda l, j: (0, 0)),
            pl.BlockSpec((1, d, tn), lambda l, j: (l, 0, j)),
            pl.BlockSpec((1, 1, tn), lambda l, j: (l, 0, j)),
        ],
        out_specs=pl.BlockSpec((1, bsz, tn), lambda l, j: (l, 0, j)),
        out_shape=jax.ShapeDtypeStruct((depth, bsz, n), F32),
        compiler_params=_cparams(("arbitrary", "arbitrary")),
    )(c, w_mod, b_mod.reshape(depth, 1, n))


def _proj_kernel(x_ref, mod_ref, w_ref, o1_ref, o2_ref, o3_ref):
    x = x_ref[...]
    sh = mod_ref[0, 0:1, :]
    sc = mod_ref[0, 1:2, :]
    h = (x * (1.0 + sc) + sh).astype(BF16)
    o1_ref[...] = jnp.dot(h, w_ref[:, 0:N1], preferred_element_type=F32).astype(BF16)
    o2_ref[...] = jnp.dot(h, w_ref[:, N1:N1 + N2], preferred_element_type=F32).astype(BF16)
    o3_ref[...] = jnp.dot(h, w_ref[:, N1 + N2:N1 + N2 + N3], preferred_element_type=F32)


def _proj(x2d, mod_l, w_in_r, seq, tm):
    m, d = x2d.shape
    tpb = seq // tm
    nw = N1 + N2 + N3
    return pl.pallas_call(
        _proj_kernel,
        grid=(m // tm,),
        in_specs=[
            pl.BlockSpec((tm, d), lambda i: (i, 0)),
            pl.BlockSpec((1, 6, d), lambda i: (i // tpb, 0, 0)),
            pl.BlockSpec((d, nw), lambda i: (0, 0)),
        ],
        out_specs=[
            pl.BlockSpec((tm, N1), lambda i: (i, 0)),
            pl.BlockSpec((tm, N2), lambda i: (i, 0)),
            pl.BlockSpec((tm, N3), lambda i: (i, 0)),
        ],
        out_shape=[
            jax.ShapeDtypeStruct((m, N1), BF16),
            jax.ShapeDtypeStruct((m, N2), BF16),
            jax.ShapeDtypeStruct((m, N3), F32),
        ],
        compiler_params=_cparams(("arbitrary",)),
    )(x2d, mod_l, w_in_r)


def _dsa_kernel(q_ref, kv_ref, w_ref, o_ref, keys_ref, m_ref, l_ref, acc_ref, *, T, tk, k_top):
    i = pl.program_id(1)
    t0 = i * T
    log_tk = tk.bit_length() - 1
    nk = lax.shift_right_logical(t0 + T + tk - 1, log_tk)

    row = lax.broadcasted_iota(I32, (T, tk), 0)
    col = lax.broadcasted_iota(I32, (T, tk), 1)
    rmc = row - col

    w = w_ref[:, IW_OFF:IW_OFF + IDX_HEADS] * ((IDX_HEADS ** -0.5) * (IDX_DIM ** -0.5))
    qi = [q_ref[:, 512 + h * IDX_DIM:512 + (h + 1) * IDX_DIM] for h in range(IDX_HEADS)]
    nt = (((1,), (1,)), ((), ()))

    def p1_body(j, carry):
        off = pl.multiple_of(j * tk, tk)
        ki = kv_ref[pl.ds(off, tk), 128:192]
        acc = jnp.zeros((T, tk), F32)
        for h in range(IDX_HEADS):
            s = lax.dot_general(qi[h], ki, nt, preferred_element_type=F32)
            acc = acc + w[:, h:h + 1] * jnp.maximum(s, 0.0)
        bits = pltpu.bitcast(acc, I32)
        key = jnp.where(bits >= 0, bits, bits ^ 0x7FFFFFFF)
        valid = (rmc + (t0 - j * tk)) >= 0
        keys_ref[j] = jnp.where(valid, key, INT_MIN)
        return carry

    lax.fori_loop(0, nk, p1_body, 0)

    kf = float(k_top)

    def bit_body(bi, mu):
        bit = 31 - bi
        cand_u = mu | lax.shift_left(jnp.int32(1), bit)
        cand_s = cand_u ^ INT_MIN

        def cnt_body(j, c):
            tile = keys_ref[j]
            for cc in range(tk // 128):
                c = c + jnp.where(tile[:, cc * 128:(cc + 1) * 128] >= cand_s, 1.0, 0.0)
            return c

        c = lax.fori_loop(0, nk, cnt_body, jnp.zeros((T, 128), F32))
        cnt = jnp.sum(c, axis=1, keepdims=True)
        return jnp.where(cnt >= kf, cand_u, mu)

    mu = lax.fori_loop(0, 32, bit_body, jnp.zeros((T, 1), I32))
    thr = jnp.maximum(mu ^ INT_MIN, INT_MIN + 1)

    m_ref[...] = jnp.full(m_ref.shape, NEG_BIG, F32)
    l_ref[...] = jnp.zeros(l_ref.shape, F32)
    acc_ref[...] = jnp.zeros(acc_ref.shape, F32)
    qa = [q_ref[:, h * A_HEAD_DIM:(h + 1) * A_HEAD_DIM] * jnp.asarray(A_HEAD_DIM ** -0.5, BF16)
          for h in range(A_HEADS)]

    def p3_body(j, carry):
        off = pl.multiple_of(j * tk, tk)
        kt = kv_ref[pl.ds(off, tk), 0:64]
        vt = kv_ref[pl.ds(off, tk), 64:128]
        sel = keys_ref[j] >= thr
        dist = (rmc + (t0 - j * tk)).astype(F32)
        for h in range(A_HEADS):
            slope = 2.0 ** (-8.0 * (h + 1) / A_HEADS)
            s = lax.dot_general(qa[h], kt, nt, preferred_element_type=F32)
            s = jnp.where(sel, s - slope * dist, NEG_BIG)
            m_old = m_ref[h]
            m_new = jnp.maximum(m_old, jnp.max(s, axis=1, keepdims=True))
            alpha = jnp.exp(m_old - m_new)
            p = jnp.exp(s - m_new)
            l_ref[h] = alpha * l_ref[h] + jnp.sum(p, axis=1, keepdims=True)
            acc_ref[h] = alpha * acc_ref[h] + jnp.dot(p.astype(BF16), vt, preferred_element_type=F32)
            m_ref[h] = m_new
        return carry

    lax.fori_loop(0, nk, p3_body, 0)

    for h in range(A_HEADS):
        o_ref[:, h * A_HEAD_DIM:(h + 1) * A_HEAD_DIM] = (acc_ref[h] / l_ref[h]).astype(BF16)


def _dsa(p1, p3, bsz, seq, T, tk):
    m = p1.shape[0]
    nq = seq // T
    nkt = seq // tk
    k_top = min(TOPK_MAX, seq // 4)
    kern = functools.partial(_dsa_kernel, T=T, tk=tk, k_top=k_top)
    return pl.pallas_call(
        kern,
        grid=(bsz, nq),
        in_specs=[
            pl.BlockSpec((T, 1024), lambda b, i: (b * nq + i, 0)),
            pl.BlockSpec((seq, 256), lambda b, i: (b, 4)),
            pl.BlockSpec((T, 128), lambda b, i: (b * nq + i, 4)),
        ],
        out_specs=pl.BlockSpec((T, 512), lambda b, i: (b * nq + i, 0)),
        out_shape=jax.ShapeDtypeStruct((m, 512), BF16),
        scratch_shapes=[
            pltpu.VMEM((nkt, T, tk), I32),
            pltpu.VMEM((A_HEADS, T, 1), F32),
            pltpu.VMEM((A_HEADS, T, 1), F32),
            pltpu.VMEM((A_HEADS, T, A_HEAD_DIM), F32),
        ],
        compiler_params=_cparams(("arbitrary", "arbitrary")),
    )(p1, p1, p3)


def _gla_kernel(p2_ref, p3_ref, wa2_ref, ba_ref, g_ref, o_ref, st_ref, *, R):
    C = GLA_CHUNK

    @pl.when(pl.program_id(1) == 0)
    def _():
        st_ref[...] = jnp.zeros(st_ref.shape, F32)

    ba = p3_ref[:, 512 + BA_OFF:512 + BA_OFF + GATE_RANK]
    xg = jnp.dot(ba, wa2_ref[...], precision=HIGHEST, preferred_element_type=F32) + ba_ref[...]
    log_a = (jnp.minimum(xg, 0.0) - jnp.log(1.0 + jnp.exp(-jnp.abs(xg)))) * (1.0 / GATE_TAU)

    r_i = lax.broadcasted_iota(I32, (C, C), 0)
    c_i = lax.broadcasted_iota(I32, (C, C), 1)
    tri_b = r_i >= c_i
    tri_f = tri_b.astype(F32)
    nt = (((1,), (1,)), ((), ()))
    tn = (((0,), (0,)), ((), ()))
    gamma = g_ref[...]

    for c in range(R // C):
        rows = slice(c * C, (c + 1) * C)
        g = log_a[rows, :]
        b = jnp.dot(tri_f, g, precision=HIGHEST, preferred_element_type=F32)
        b_last = b[C - 1:C, :]
        b_mid = b[C // 2 - 1:C // 2, :]
        q = p2_ref[rows, 0:256].astype(F32) * (B_KEY_DIM ** -0.5)
        k = p2_ref[rows, 256:512].astype(F32)
        qe = (q * jnp.exp(b)).astype(BF16)
        qm = (q * jnp.exp(b - b_mid)).astype(BF16)
        km = (k * jnp.exp(b_mid - b)).astype(BF16)
        kl = (k * jnp.exp(b_last - b)).astype(BF16)
        dec = jnp.exp(b_last)
        for h in range(B_HEADS):
            ks = slice(h * B_KEY_DIM, (h + 1) * B_KEY_DIM)
            vs = slice(512 + h * B_VAL_DIM, 512 + (h + 1) * B_VAL_DIM)
            v = p2_ref[rows, vs]
            st = st_ref[h]
            a = lax.dot_general(qm[:, ks], km[:, ks], nt, preferred_element_type=F32)
            a = jnp.where(tri_b, a, 0.0).astype(BF16)
            o = lax.dot_general(qe[:, ks], st.astype(BF16), nt, preferred_element_type=F32)
            o = o + jnp.dot(a, v, preferred_element_type=F32)
            st_ref[h] = st * dec[:, ks] + lax.dot_general(v, kl[:, ks], tn, preferred_element_type=F32)
            ms = jnp.mean(o * o, axis=1, keepdims=True)
            on = o * lax.rsqrt(ms + NORM_EPS) * gamma
            gate = p3_ref[rows, h * B_VAL_DIM:(h + 1) * B_VAL_DIM]
            o_ref[rows, h * B_VAL_DIM:(h + 1) * B_VAL_DIM] = (on * (gate * _sigmoid(gate))).astype(BF16)


def _gla(p2, p3, w_a2, b_a, gamma, bsz, seq, R):
    m = p2.shape[0]
    nr = seq // R
    kern = functools.partial(_gla_kernel, R=R)
    return pl.pallas_call(
        kern,
        grid=(bsz, nr),
        in_specs=[
            pl.BlockSpec((R, N2), lambda b, i: (b * nr + i, 0)),
            pl.BlockSpec((R, N3), lambda b, i: (b * nr + i, 0)),
            pl.BlockSpec((GATE_RANK, 256), lambda b, i: (0, 0)),
            pl.BlockSpec((1, 256), lambda b, i: (0, 0)),
            pl.BlockSpec((1, B_VAL_DIM), lambda b, i: (0, 0)),
        ],
        out_specs=pl.BlockSpec((R, 512), lambda b, i: (b * nr + i, 0)),
        out_shape=jax.ShapeDtypeStruct((m, 512), BF16),
        scratch_shapes=[pltpu.VMEM((B_HEADS, B_VAL_DIM, B_KEY_DIM), F32)],
        compiler_params=_cparams(("arbitrary", "arbitrary")),
    )(p2, p3, w_a2, b_a.reshape(1, 256), gamma.reshape(1, B_VAL_DIM))


def _resid_ln(x, y, gate, ln_g, ln_b, alpha):
    z = alpha * x + (1.0 + gate) * y
    mu = jnp.mean(z, axis=1, keepdims=True)
    zc = z - mu
    var = jnp.mean(zc * zc, axis=1, keepdims=True)
    return zc * lax.rsqrt(var + NORM_EPS) * ln_g + ln_b


def _outproj_kernel(ya_ref, ob_ref, x_ref, w_ref, mod_ref, lng_ref, lnb_ref, *rest, alpha, routed):
    if routed:
        wr_ref, xo_ref, h_ref, lg_ref = rest
    else:
        xo_ref, h_ref = rest
    y = jnp.dot(ya_ref[...], w_ref[0:512, :], preferred_element_type=F32)
    y = y + jnp.dot(ob_ref[...], w_ref[512:1024, :], preferred_element_type=F32)
    xn = _resid_ln(x_ref[...], y, mod_ref[0, 2:3, :], lng_ref[...], lnb_ref[...], alpha)
    xo_ref[...] = xn
    h = xn * (1.0 + mod_ref[0, 4:5, :]) + mod_ref[0, 3:4, :]
    if routed:
        h_ref[...] = h
        lg_ref[...] = jnp.dot(h, wr_ref[...], precision=HIGHEST, preferred_element_type=F32)
    else:
        h_ref[...] = h.astype(BF16)


def _outproj(ya, ob, x2d, w_out, mod_l, ln_g, ln_b, w_router_p, seq, tm, alpha):
    m, d = x2d.shape
    tpb = seq // tm
    routed = w_router_p is not None
    kern = functools.partial(_outproj_kernel, alpha=alpha, routed=routed)
    in_specs = [
        pl.BlockSpec((tm, 512), lambda i: (i, 0)),
        pl.BlockSpec((tm, 512), lambda i: (i, 0)),
        pl.BlockSpec((tm, d), lambda i: (i, 0)),
        pl.BlockSpec((d, d), lambda i: (0, 0)),
        pl.BlockSpec((1, 6, d), lambda i: (i // tpb, 0, 0)),
        pl.BlockSpec((1, d), lambda i: (0, 0)),
        pl.BlockSpec((1, d), lambda i: (0, 0)),
    ]
    args = [ya, ob, x2d, w_out, mod_l, ln_g.reshape(1, d), ln_b.reshape(1, d)]
    out_specs = [pl.BlockSpec((tm, d), lambda i: (i, 0)), pl.BlockSpec((tm, d), lambda i: (i, 0))]
    out_shape = [jax.ShapeDtypeStruct((m, d), F32), jax.ShapeDtypeStruct((m, d), F32 if routed else BF16)]
    if routed:
        in_specs.append(pl.BlockSpec((d, 128), lambda i: (0, 0)))
        args.append(w_router_p)
        out_specs.append(pl.BlockSpec((tm, 128), lambda i: (i, 0)))
        out_shape.append(jax.ShapeDtypeStruct((m, 128), F32))
    return pl.pallas_call(
        kern,
        grid=(m // tm,),
        in_specs=in_specs,
        out_specs=out_specs,
        out_shape=out_shape,
        compiler_params=_cparams(("arbitrary",)),
    )(*args)


def _swiglu_acc(h, wg_ref, wu_ref, wd_ref, acc_ref, tf, widx):
    c0 = 0
    while c0 < tf:
        c1 = min(c0 + 512, tf)
        if widx is None:
            wg, wu, wd = wg_ref[:, c0:c1], wu_ref[:, c0:c1], wd_ref[c0:c1, :]
        else:
            wg, wu, wd = wg_ref[0, :, c0:c1], wu_ref[0, :, c0:c1], wd_ref[0, c0:c1, :]
        g = jnp.dot(h, wg, preferred_element_type=F32)
        u = jnp.dot(h, wu, preferred_element_type=F32)
        a = (g * _sigmoid(g) * u).astype(BF16)
        acc_ref[...] += jnp.dot(a, wd, preferred_element_type=F32)
        c0 = c1


def _ffn_dense_kernel(h_ref, x_ref, wg_ref, wu_ref, wd_ref, mod_ref, lng_ref, lnb_ref, xo_ref, acc_ref,
                      *, tf, alpha):
    f = pl.program_id(1)

    @pl.when(f == 0)
    def _():
        acc_ref[...] = jnp.zeros(acc_ref.shape, F32)

    _swiglu_acc(h_ref[...], wg_ref, wu_ref, wd_ref, acc_ref, tf, None)

    @pl.when(f == pl.num_programs(1) - 1)
    def _():
        xo_ref[...] = _resid_ln(x_ref[...], acc_ref[...], mod_ref[0, 5:6, :], lng_ref[...], lnb_ref[...], alpha)


def _ffn_dense(h, x2d, wg, wu, wd, mod_l, ln_g, ln_b, seq, tm, tf, alpha):
    m, d = x2d.shape
    dff = wg.shape[1]
    tpb = seq // tm
    kern = functools.partial(_ffn_dense_kernel, tf=tf, alpha=alpha)
    return pl.pallas_call(
        kern,
        grid=(m // tm, dff // tf),
        in_specs=[
            pl.BlockSpec((tm, d), lambda i, f: (i, 0)),
            pl.BlockSpec((tm, d), lambda i, f: (i, 0)),
            pl.BlockSpec((d, tf), lambda i, f: (0, f)),
            pl.BlockSpec((d, tf), lambda i, f: (0, f)),
            pl.BlockSpec((tf, d), lambda i, f: (f, 0)),
            pl.BlockSpec((1, 6, d), lambda i, f: (i // tpb, 0, 0)),
            pl.BlockSpec((1, d), lambda i, f: (0, 0)),
            pl.BlockSpec((1, d), lambda i, f: (0, 0)),
        ],
        out_specs=pl.BlockSpec((tm, d), lambda i, f: (i, 0)),
        out_shape=jax.ShapeDtypeStruct((m, d), F32),
        scratch_shapes=[pltpu.VMEM((tm, d), F32)],
        compiler_params=_cparams(("arbitrary", "arbitrary")),
    )(h, x2d, wg, wu, wd, mod_l, ln_g.reshape(1, d), ln_b.reshape(1, d))


def _ffn_grouped_kernel(te_ref, nu_ref, xs_ref, wg_ref, wu_ref, wd_ref, yo_ref, acc_ref, *, tf):
    i = pl.program_id(0)
    f = pl.program_id(1)
    used = i < nu_ref[0]

    @pl.when(jnp.logical_and(used, f == 0))
    def _():
        acc_ref[...] = jnp.zeros(acc_ref.shape, F32)

    @pl.when(used)
    def _():
        _swiglu_acc(xs_ref[...].astype(BF16), wg_ref, wu_ref, wd_ref, acc_ref, tf, 0)

    @pl.when(jnp.logical_and(used, f == pl.num_programs(1) - 1))
    def _():
        yo_ref[...] = acc_ref[...]


def _ffn_grouped(xs, tile_expert, n_used, wg, wu, wd, tm, tf):
    n_rows, d = xs.shape
    dff = wg.shape[2]
    nf = dff // tf

    def w_map(i, f, te, nu):
        return (te[i], 0, jnp.where(i < nu[0], f, nf - 1))

    def wd_map(i, f, te, nu):
        return (te[i], jnp.where(i < nu[0], f, nf - 1), 0)

    kern = functools.partial(_ffn_grouped_kernel, tf=tf)
    return pl.pallas_call(
        kern,
        grid_spec=pltpu.PrefetchScalarGridSpec(
            num_scalar_prefetch=2,
            grid=(n_rows // tm, nf),
            in_specs=[
                pl.BlockSpec((tm, d), lambda i, f, te, nu: (i, 0)),
                pl.BlockSpec((1, d, tf), w_map),
                pl.BlockSpec((1, d, tf), w_map),
                pl.BlockSpec((1, tf, d), wd_map),
            ],
            out_specs=pl.BlockSpec((tm, d), lambda i, f, te, nu: (i, 0)),
            scratch_shapes=[pltpu.VMEM((tm, d), F32)],
        ),
        out_shape=jax.ShapeDtypeStruct((n_rows, d), F32),
        compiler_params=_cparams(("arbitrary", "arbitrary")),
    )(tile_expert, n_used, xs, wg, wu, wd)


def _row_copy(src, dst, sem):
    return pltpu.make_async_copy(src, dst, sem)


def _scatter_kernel(dest_ref, h_ref, xs_ref, sem, *, ts):
    def start(r, carry):
        for s in range(2):
            _row_copy(h_ref.at[pl.ds(r, 1), :], xs_ref.at[pl.ds(dest_ref[0, 0, 2 * r + s], 1), :], sem).start()
        return carry

    lax.fori_loop(0, ts, start, 0)

    def wait(r, carry):
        for s in range(2):
            _row_copy(h_ref.at[pl.ds(0, 1), :], xs_ref.at[pl.ds(0, 1), :], sem).wait()
        return carry

    lax.fori_loop(0, ts, wait, 0)


def _scatter_rows(h, dest, n_rows, ts):
    n_tok, d = h.shape
    kern = functools.partial(_scatter_kernel, ts=ts)
    return pl.pallas_call(
        kern,
        grid=(n_tok // ts,),
        in_specs=[
            pl.BlockSpec((1, 1, 2 * ts), lambda i: (i, 0, 0), memory_space=pltpu.SMEM),
            pl.BlockSpec((ts, d), lambda i: (i, 0)),
        ],
        out_specs=pl.BlockSpec(memory_space=pl.ANY),
        out_shape=jax.ShapeDtypeStruct((n_rows, d), F32),
        scratch_shapes=[pltpu.SemaphoreType.DMA(())],
        compiler_params=_cparams(("arbitrary",)),
    )(dest.reshape(n_tok // ts, 1, 2 * ts), h)


def _combine_kernel(dest_ref, gate_ref, x_ref, mod_ref, lng_ref, lnb_ref, yb_ref, xo_ref, r0_ref, r1_ref, sem,
                    *, ts, alpha):
    def start(r, carry):
        _row_copy(yb_ref.at[pl.ds(dest_ref[0, 0, 2 * r], 1), :], r0_ref.at[pl.ds(r, 1), :], sem).start()
        _row_copy(yb_ref.at[pl.ds(dest_ref[0, 0, 2 * r + 1], 1), :], r1_ref.at[pl.ds(r, 1), :], sem).start()
        return carry

    lax.fori_loop(0, ts, start, 0)

    def wait(r, carry):
        for s in range(2):
            _row_copy(yb_ref.at[pl.ds(0, 1), :], r0_ref.at[pl.ds(0, 1), :], sem).wait()
        return carry

    lax.fori_loop(0, ts, wait, 0)

    y = gate_ref[:, 0:1] * r0_ref[...] + gate_ref[:, 1:2] * r1_ref[...]
    xo_ref[...] = _resid_ln(x_ref[...], y, mod_ref[0, 5:6, :], lng_ref[...], lnb_ref[...], alpha)


def _combine(dest, gates, x2d, mod_l, ln_g, ln_b, yb, seq, ts, alpha):
    n_tok, d = x2d.shape
    tpb = seq // ts
    kern = functools.partial(_combine_kernel, ts=ts, alpha=alpha)
    return pl.pallas_call(
        kern,
        grid=(n_tok // ts,),
        in_specs=[
            pl.BlockSpec((1, 1, 2 * ts), lambda i: (i, 0, 0), memory_space=pltpu.SMEM),
            pl.BlockSpec((ts, 2), lambda i: (i, 0)),
            pl.BlockSpec((ts, d), lambda i: (i, 0)),
            pl.BlockSpec((1, 6, d), lambda i: (i // tpb, 0, 0)),
            pl.BlockSpec((1, d), lambda i: (0, 0)),
            pl.BlockSpec((1, d), lambda i: (0, 0)),
            pl.BlockSpec(memory_space=pl.ANY),
        ],
        out_specs=pl.BlockSpec((ts, d), lambda i: (i, 0)),
        out_shape=jax.ShapeDtypeStruct((n_tok, d), F32),
        scratch_shapes=[pltpu.VMEM((ts, d), F32), pltpu.VMEM((ts, d), F32), pltpu.SemaphoreType.DMA(())],
        compiler_params=_cparams(("arbitrary",)),
    )(dest.reshape(n_tok // ts, 1, 2 * ts), gates, x2d, mod_l, ln_g.reshape(1, d), ln_b.reshape(1, d), yb)


def _route(logits, tm_e):
    n_tok = logits.shape[0]
    top_val, top_idx = lax.top_k(logits, 2)
    gates = jax.nn.softmax(top_val, axis=-1)
    e_flat = top_idx.reshape(-1).astype(I32)
    onehot = (e_flat[:, None] == jnp.arange(N_EXPERTS, dtype=I32)[None, :]).astype(I32)
    csum = jnp.cumsum(onehot, axis=0)
    counts = csum[-1]
    rank = jnp.sum(csum * onehot, axis=1) - 1
    padded = ((counts + tm_e - 1) // tm_e) * tm_e
    pend = jnp.cumsum(padded)
    pstart = pend - padded
    dest = (pstart[e_flat] + rank).astype(I32)
    n_tiles = (2 * n_tok) // tm_e + N_EXPERTS
    n_used = (pend[-1] // tm_e).astype(I32)
    tile_start = jnp.arange(n_tiles, dtype=I32) * tm_e
    tile_expert = jnp.minimum(jnp.searchsorted(pend, tile_start, side='right'), N_EXPERTS - 1).astype(I32)
    last_e = tile_expert[jnp.maximum(n_used - 1, 0)]
    tile_expert = jnp.where(jnp.arange(n_tiles) < n_used, tile_expert, last_e)
    return dest, gates, tile_expert, n_used.reshape(1), n_tiles


def _reorder_w_in(w_in_l):
    widths = (512, 64, 64, 512, 64, 8, 256, 256, 512, 512, 16)
    offs = [0]
    for wd_ in widths:
        offs.append(offs[-1] + wd_)
    seg = [w_in_l[:, offs[k]:offs[k + 1]] for k in range(len(widths))]
    aq, ak, av, iq, ik, iw, bq, bk, bv, bg, ba = seg
    d = w_in_l.shape[0]
    z = lambda n: jnp.zeros((d, n), w_in_l.dtype)
    return jnp.concatenate([aq, iq, ak, av, ik, z(64), bq, bk, bv, bg, iw, ba, z(104)], axis=1).astype(BF16)


def kernel(x, c, w_in, w_a2, b_a, gla_norm_g, w_out, w_mod, b_mod, ln_g, ln_b, ffn_w_gate, ffn_w_up, ffn_w_down,
           w_router, b_router, moe_w_gate, moe_w_up, moe_w_down):
    bsz, seq, d = x.shape
    depth = w_in.shape[0]
    dff = ffn_w_gate.shape[2]
    m = bsz * seq
    alpha = (2 * depth) ** 0.25

    tm_p = min(512, seq)
    tm_f = min(1024, seq)
    tm_e = min(1024, seq)
    T = min(128, seq)
    tk = min(512, seq)
    R = min(512, seq)
    ts = min(256, seq)
    tf = dff // 2 if (dff // 2) % 128 == 0 else dff

    mod = _modulation(c, w_mod, b_mod).reshape(depth, bsz, 6, d)
    x2d = x.reshape(m, d)
    for l in range(depth):
        mod_l = mod[l]
        p1, p2, p3 = _proj(x2d, mod_l, _reorder_w_in(w_in[l]), seq, tm_p)
        ya = _dsa(p1, p3, bsz, seq, T, tk)
        ob = _gla(p2, p3, w_a2[l], b_a[l], gla_norm_g[l], bsz, seq, R)
        j = l // 2
        if l % 2 == 0:
            x1, h2 = _outproj(ya, ob, x2d, w_out[l].astype(BF16), mod_l, ln_g[l, 0], ln_b[l, 0], None,
                              seq, tm_p, alpha)
            x2d = _ffn_dense(h2, x1, ffn_w_gate[j].astype(BF16), ffn_w_up[j].astype(BF16),
                             ffn_w_down[j].astype(BF16), mod_l, ln_g[l, 1], ln_b[l, 1], seq, tm_f, tf, alpha)
        else:
            wr = jnp.zeros((d, 128), F32).at[:, :N_EXPERTS].set(w_router[j])
            x1, h2, lg = _outproj(ya, ob, x2d, w_out[l].astype(BF16), mod_l, ln_g[l, 0], ln_b[l, 0], wr,
                                  seq, tm_p, alpha)
            logits = lg[:, :N_EXPERTS] + b_router[j][None, :]
            dest, gates, tile_expert, n_used, n_tiles = _route(logits, tm_e)
            xs = _scatter_rows(h2, dest, n_tiles * tm_e, ts)
            yb = _ffn_grouped(xs, tile_expert, n_used, moe_w_gate[j].astype(BF16), moe_w_up[j].astype(BF16),
                              moe_w_down[j].astype(BF16), tm_e, tf)
            x2d = _combine(dest, gates, x1, mod_l, ln_g[l, 1], ln_b[l, 1], yb, seq, ts, alpha)
    return x2d.reshape(bsz, seq, d)
```

```python
import functools

import jax
import jax.numpy as jnp
from jax import lax
from jax.experimental import pallas as pl
from jax.experimental.pallas import tpu as pltpu

F32 = jnp.float32
BF16 = jnp.bfloat16
I32 = jnp.int32
HIGHEST = lax.Precision.HIGHEST

A_HEADS = 8
A_HEAD_DIM = 64
IDX_HEADS = 8
IDX_DIM = 64
TOPK_MAX = 256
B_HEADS = 4
B_KEY_DIM = 64
B_VAL_DIM = 128
GATE_RANK = 16
GATE_TAU = 16.0
GLA_CHUNK = 64
N_EXPERTS = 8
NORM_EPS = 1e-5

N1, N2, N3 = 1280, 1024, 640
IW_OFF = 0
BA_OFF = 8

LANE_GROUP = 256
VMEM_LIMIT = 56 * 1024 * 1024
INT_MIN = -2147483648
NEG_BIG = -1e30


def _sigmoid(x):
    return 1.0 / (1.0 + jnp.exp(-x))


def _cparams(sem):
    return pltpu.CompilerParams(dimension_semantics=sem, vmem_limit_bytes=VMEM_LIMIT)


def _mod_kernel(c_ref, w_ref, b_ref, o_ref):
    c = c_ref[...]
    ca = c * _sigmoid(c)
    o_ref[0] = jnp.dot(ca, w_ref[0], precision=HIGHEST, preferred_element_type=F32) + b_ref[0]


def _modulation(c, w_mod, b_mod):
    depth, d, n = w_mod.shape
    bsz = c.shape[0]
    tn = 1536 if n % 1536 == 0 else n
    return pl.pallas_call(
        _mod_kernel,
        grid=(depth, n // tn),
        in_specs=[
            pl.BlockSpec((bsz, d), lambda l, j: (0, 0)),
            pl.BlockSpec((1, d, tn), lambda l, j: (l, 0, j)),
            pl.BlockSpec((1, 1, tn), lambda l, j: (l, 0, j)),
        ],
        out_specs=pl.BlockSpec((1, bsz, tn), lambda l, j: (l, 0, j)),
        out_shape=jax.ShapeDtypeStruct((depth, bsz, n), F32),
        compiler_params=_cparams(("arbitrary", "arbitrary")),
    )(c, w_mod, b_mod.reshape(depth, 1, n))


def _proj_kernel(x_ref, mod_ref, w_ref, wvt_ref, o1_ref, o2_ref, o3_ref, vt_ref, *, tk):
    x = x_ref[...]
    sh = mod_ref[0, 0:1, :]
    sc = mod_ref[0, 1:2, :]
    h = (x * (1.0 + sc) + sh).astype(BF16)
    o1_ref[...] = jnp.dot(h, w_ref[:, 0:N1], preferred_element_type=F32).astype(BF16)
    o2_ref[...] = jnp.dot(h, w_ref[:, N1:N1 + N2], preferred_element_type=F32).astype(BF16)
    o3_ref[...] = jnp.dot(h, w_ref[:, N1 + N2:N1 + N2 + N3], preferred_element_type=F32)
    nt = (((1,), (1,)), ((), ()))
    for c in range(vt_ref.shape[0]):
        vt_ref[c] = lax.dot_general(wvt_ref[...], h[c * tk:(c + 1) * tk, :], nt,
                                    preferred_element_type=F32).astype(BF16)


def _proj(x2d, mod_l, w_in_r, w_vt, seq, tm, tk):
    m, d = x2d.shape
    tpb = seq // tm
    nw = N1 + N2 + N3
    kern = functools.partial(_proj_kernel, tk=tk)
    return pl.pallas_call(
        kern,
        grid=(m // tm,),
        in_specs=[
            pl.BlockSpec((tm, d), lambda i: (i, 0)),
            pl.BlockSpec((1, 6, d), lambda i: (i // tpb, 0, 0)),
            pl.BlockSpec((d, nw), lambda i: (0, 0)),
            pl.BlockSpec((A_HEAD_DIM, d), lambda i: (0, 0)),
        ],
        out_specs=[
            pl.BlockSpec((tm, N1), lambda i: (i, 0)),
            pl.BlockSpec((tm, N2), lambda i: (i, 0)),
            pl.BlockSpec((tm, N3), lambda i: (i, 0)),
            pl.BlockSpec((tm // tk, A_HEAD_DIM, tk), lambda i: (i, 0, 0)),
        ],
        out_shape=[
            jax.ShapeDtypeStruct((m, N1), BF16),
            jax.ShapeDtypeStruct((m, N2), BF16),
            jax.ShapeDtypeStruct((m, N3), F32),
            jax.ShapeDtypeStruct((m // tk, A_HEAD_DIM, tk), BF16),
        ],
        compiler_params=_cparams(("arbitrary",)),
    )(x2d, mod_l, w_in_r, w_vt)


def _dsa_kernel(q_ref, kv_ref, vt_ref, w_ref, qext_ref, postab_ref, o_ref,
                kx_ref, qx_ref, qis_ref, keys_ref, m_ref, l_ref, acc_ref, *, T, tk, k_top):
    i = pl.program_id(1)
    t0 = i * T
    log_tk = tk.bit_length() - 1
    nk = lax.shift_right_logical(t0 + T + tk - 1, log_tk)
    G = LANE_GROUP // T
    NG = A_HEADS // G
    nt = (((1,), (1,)), ((), ()))

    @pl.when(i == 0)
    def _():
        kx_ref[...] = kv_ref[:, 0:128] + postab_ref[...]
        qx_ref[:, 64:128] = qext_ref[...]

    for h in range(A_HEADS):
        qx_ref[h * T:(h + 1) * T, 0:64] = q_ref[:, h * A_HEAD_DIM:(h + 1) * A_HEAD_DIM]
        qis_ref[h * T:(h + 1) * T, :] = q_ref[:, 512 + h * IDX_DIM:512 + (h + 1) * IDX_DIM]

    row = lax.broadcasted_iota(I32, (tk, T), 0)
    col = lax.broadcasted_iota(I32, (tk, T), 1)
    cmr = col - row

    w_t = jnp.transpose(w_ref[...]) * ((IDX_HEADS ** -0.5) * (IDX_DIM ** -0.5))
    w_rows = [w_t[IW_OFF + h:IW_OFF + h + 1, :] for h in range(IDX_HEADS)]

    def p1_body(j, carry):
        off = pl.multiple_of(j * tk, tk)
        ki = kv_ref[pl.ds(off, tk), 128:192]
        acc = jnp.zeros((tk, T), F32)
        for g in range(NG):
            s = lax.dot_general(ki, qis_ref[g * LANE_GROUP:(g + 1) * LANE_GROUP, :], nt,
                                preferred_element_type=F32)
            for hh in range(G):
                acc = acc + w_rows[g * G + hh] * jnp.maximum(s[:, hh * T:(hh + 1) * T], 0.0)
        bits = pltpu.bitcast(acc, I32)
        key = jnp.where(bits >= 0, bits, bits ^ 0x7FFFFFFF)
        valid = (cmr + (t0 - j * tk)) >= 0
        keys_ref[j] = jnp.where(valid, key, INT_MIN).reshape(tk // 8, 8, T)
        return carry

    lax.fori_loop(0, nk, p1_body, 0)

    kf = float(k_top)

    def count_ge(cand_s):
        cand_b = jnp.broadcast_to(cand_s, (8, T))[None]

        def cnt_body(j, c):
            x = jnp.where(keys_ref[j] >= cand_b, 1.0, 0.0)
            return c + jnp.sum(x.reshape(4, tk // 32, 8, T), axis=1)

        c = lax.fori_loop(0, nk, cnt_body, jnp.zeros((4, 8, T), F32))
        return jnp.sum(jnp.sum(c, axis=0), axis=0, keepdims=True)

    def bit_body(bi, carry):
        mu, cnt_mu = carry
        cand_u = mu | lax.shift_left(jnp.int32(1), 31 - bi)
        cnt = count_ge(cand_u ^ INT_MIN)
        ok = cnt >= kf
        return jnp.where(ok, cand_u, mu), jnp.where(ok, cnt, cnt_mu)

    mu, cnt_ge = lax.fori_loop(0, 32, bit_body, (jnp.zeros((1, T), I32), jnp.zeros((1, T), F32)))
    thr = jnp.maximum(mu ^ INT_MIN, INT_MIN + 1)

    surplus = jnp.where(mu != 0, cnt_ge - kf, 0.0)
    has_ties = jnp.max(surplus) > 0.0

    @pl.when(has_ties)
    def _():
        need = kf - count_ge(thr + 1)
        thr_b = jnp.broadcast_to(thr, (tk, T))
        r2 = lax.broadcasted_iota(I32, (tk, tk), 0)
        c2 = lax.broadcasted_iota(I32, (tk, tk), 1)
        tri = jnp.where(r2 >= c2, 1.0, 0.0).astype(BF16)

        def fix_body(j, base):
            tile = keys_ref[j].reshape(tk, T)
            tie = tile == thr_b
            pref = jnp.dot(tri, jnp.where(tie, 1.0, 0.0).astype(BF16), preferred_element_type=F32) + base
            drop = jnp.logical_and(tie, pref > need)
            keys_ref[j] = jnp.where(drop, tile - 1, tile).reshape(tk // 8, 8, T)
            return pref[tk - 1:tk, :]

        lax.fori_loop(0, nk, fix_body, jnp.zeros((1, T), F32))

    m_ref[...] = jnp.full(m_ref.shape, NEG_BIG, F32)
    l_ref[...] = jnp.zeros(l_ref.shape, F32)
    acc_ref[...] = jnp.zeros(acc_ref.shape, F32)
    thr_b = jnp.broadcast_to(thr, (tk, T))

    def p3_body(j, carry):
        off = pl.multiple_of(j * tk, tk)
        kx = kx_ref[pl.ds(off, tk), :]
        vt = vt_ref[j]
        sel = keys_ref[j].reshape(tk, T) >= thr_b
        sel_g = jnp.concatenate([sel] * G, axis=1) if G > 1 else sel
        s_all = [lax.dot_general(kx, qx_ref[g * LANE_GROUP:(g + 1) * LANE_GROUP, :], nt,
                                 preferred_element_type=F32) for g in range(NG)]
        for g in range(NG):
            s = jnp.where(sel_g, s_all[g], NEG_BIG)
            m_old = m_ref[g]
            m_new = jnp.maximum(m_old, jnp.max(s, axis=0, keepdims=True))
            alpha = jnp.exp2(m_old - m_new)
            p = jnp.exp2(s - m_new)
            l_ref[g] = alpha * l_ref[g] + jnp.sum(p, axis=0, keepdims=True)
            acc_ref[g] = alpha * acc_ref[g] + jnp.dot(vt, p.astype(BF16), preferred_element_type=F32)
            m_ref[g] = m_new
        return carry

    lax.fori_loop(0, nk, p3_body, 0)

    def head_out(h):
        g, hh = h // G, h % G
        return acc_ref[g][:, hh * T:(hh + 1) * T] / l_ref[g][:, hh * T:(hh + 1) * T]

    for h in range(0, A_HEADS, 2):
        pair = jnp.concatenate([head_out(h), head_out(h + 1)], axis=0)
        o_ref[:, h * A_HEAD_DIM:(h + 2) * A_HEAD_DIM] = jnp.transpose(pair).astype(BF16)


def _alibi_tables(seq, T):
    log2e = 1.4426950408889634
    pos = jnp.arange(seq, dtype=I32)
    hi = (pos // 64).astype(F32)
    lo = (pos % 64).astype(F32)
    z = jnp.zeros((seq,), F32)
    pos_cols = [z] * 64 + [hi, hi, hi, lo, lo, lo] + [z] * 58
    postab = jnp.stack(pos_cols, axis=1).astype(BF16)
    rows = []
    for h in range(A_HEADS):
        c = jnp.asarray(2.0 ** (-8.0 * (h + 1) / A_HEADS) * log2e, F32)
        c1 = c.astype(BF16)
        c2 = (c - c1.astype(F32)).astype(BF16)
        c3 = (c - c1.astype(F32) - c2.astype(F32)).astype(BF16)
        r = jnp.zeros((64,), BF16)
        r = r.at[0].set(c1 * 64).at[1].set(c2 * 64).at[2].set(c3 * 64).at[3].set(c1).at[4].set(c2).at[5].set(c3)
        rows.append(jnp.broadcast_to(r[None, :], (T, 64)))
    qext = jnp.concatenate(rows, axis=0)
    return postab, qext


def _dsa(p1, p3, vt, bsz, seq, T, tk):
    m = p1.shape[0]
    nq = seq // T
    nkt = seq // tk
    k_top = min(TOPK_MAX, seq // 4)
    postab, qext = _alibi_tables(seq, T)
    kern = functools.partial(_dsa_kernel, T=T, tk=tk, k_top=k_top)
    ng = A_HEADS * T // LANE_GROUP
    return pl.pallas_call(
        kern,
        grid=(bsz, nq),
        in_specs=[
            pl.BlockSpec((T, 1024), lambda b, i: (b * nq + i, 0)),
            pl.BlockSpec((seq, 256), lambda b, i: (b, 4)),
            pl.BlockSpec((nkt, A_HEAD_DIM, tk), lambda b, i: (b, 0, 0)),
            pl.BlockSpec((T, 128), lambda b, i: (b * nq + i, 4)),
            pl.BlockSpec((A_HEADS * T, 64), lambda b, i: (0, 0)),
            pl.BlockSpec((seq, 128), lambda b, i: (0, 0)),
        ],
        out_specs=pl.BlockSpec((T, 512), lambda b, i: (b * nq + i, 0)),
        out_shape=jax.ShapeDtypeStruct((m, 512), BF16),
        scratch_shapes=[
            pltpu.VMEM((seq, 128), BF16),
            pltpu.VMEM((A_HEADS * T, 128), BF16),
            pltpu.VMEM((A_HEADS * T, IDX_DIM), BF16),
            pltpu.VMEM((nkt, tk // 8, 8, T), I32),
            pltpu.VMEM((ng, 1, LANE_GROUP), F32),
            pltpu.VMEM((ng, 1, LANE_GROUP), F32),
            pltpu.VMEM((ng, A_HEAD_DIM, LANE_GROUP), F32),
        ],
        compiler_params=_cparams(("arbitrary", "arbitrary")),
    )(p1, p1, vt, p3, qext, postab)


def _gla_kernel(p2_ref, p3_ref, wa2_ref, ba_ref, g_ref, o_ref, st_ref, *, R):
    C = GLA_CHUNK

    @pl.when(pl.program_id(1) == 0)
    def _():
        st_ref[...] = jnp.zeros(st_ref.shape, F32)

    ba = p3_ref[:, 512 + BA_OFF:512 + BA_OFF + GATE_RANK]
    xg = jnp.dot(ba, wa2_ref[...], precision=HIGHEST, preferred_element_type=F32) + ba_ref[...]
    log_a = (jnp.minimum(xg, 0.0) - jnp.log(1.0 + jnp.exp(-jnp.abs(xg)))) * (1.0 / GATE_TAU)

    r_i = lax.broadcasted_iota(I32, (C, C), 0)
    c_i = lax.broadcasted_iota(I32, (C, C), 1)
    tri_b = r_i >= c_i
    tri_f = tri_b.astype(F32)
    nt = (((1,), (1,)), ((), ()))
    tn = (((0,), (0,)), ((), ()))
    gamma = g_ref[...]

    for c in range(R // C):
        rows = slice(c * C, (c + 1) * C)
        g = log_a[rows, :]
        b = jnp.dot(tri_f, g, precision=HIGHEST, preferred_element_type=F32)
        b_last = b[C - 1:C, :]
        b_mid = b[C // 2 - 1:C // 2, :]
        q = p2_ref[rows, 0:256].astype(F32) * (B_KEY_DIM ** -0.5)
        k = p2_ref[rows, 256:512].astype(F32)
        qe = (q * jnp.exp(b)).astype(BF16)
        qm = (q * jnp.exp(b - b_mid)).astype(BF16)
        km = (k * jnp.exp(b_mid - b)).astype(BF16)
        kl = (k * jnp.exp(b_last - b)).astype(BF16)
        dec = jnp.exp(b_last)
        for h in range(B_HEADS):
            ks = slice(h * B_KEY_DIM, (h + 1) * B_KEY_DIM)
            vs = slice(512 + h * B_VAL_DIM, 512 + (h + 1) * B_VAL_DIM)
            v = p2_ref[rows, vs]
            st = st_ref[h]
            a = lax.dot_general(qm[:, ks], km[:, ks], nt, preferred_element_type=F32)
            a = jnp.where(tri_b, a, 0.0).astype(BF16)
            o = lax.dot_general(qe[:, ks], st.astype(BF16), nt, preferred_element_type=F32)
            o = o + jnp.dot(a, v, preferred_element_type=F32)
            st_ref[h] = st * dec[:, ks] + lax.dot_general(v, kl[:, ks], tn, preferred_element_type=F32)
            ms = jnp.mean(o * o, axis=1, keepdims=True)
            on = o * lax.rsqrt(ms + NORM_EPS) * gamma
            gate = p3_ref[rows, h * B_VAL_DIM:(h + 1) * B_VAL_DIM]
            o_ref[rows, h * B_VAL_DIM:(h + 1) * B_VAL_DIM] = (on * (gate * _sigmoid(gate))).astype(BF16)


def _gla(p2, p3, w_a2, b_a, gamma, bsz, seq, R):
    m = p2.shape[0]
    nr = seq // R
    kern = functools.partial(_gla_kernel, R=R)
    return pl.pallas_call(
        kern,
        grid=(bsz, nr),
        in_specs=[
            pl.BlockSpec((R, N2), lambda b, i: (b * nr + i, 0)),
            pl.BlockSpec((R, N3), lambda b, i: (b * nr + i, 0)),
            pl.BlockSpec((GATE_RANK, 256), lambda b, i: (0, 0)),
            pl.BlockSpec((1, 256), lambda b, i: (0, 0)),
            pl.BlockSpec((1, B_VAL_DIM), lambda b, i: (0, 0)),
        ],
        out_specs=pl.BlockSpec((R, 512), lambda b, i: (b * nr + i, 0)),
        out_shape=jax.ShapeDtypeStruct((m, 512), BF16),
        scratch_shapes=[pltpu.VMEM((B_HEADS, B_VAL_DIM, B_KEY_DIM), F32)],
        compiler_params=_cparams(("arbitrary", "arbitrary")),
    )(p2, p3, w_a2, b_a.reshape(1, 256), gamma.reshape(1, B_VAL_DIM))


def _resid_ln(x, y, gate, ln_g, ln_b, alpha):
    z = alpha * x + (1.0 + gate) * y
    mu = jnp.mean(z, axis=1, keepdims=True)
    zc = z - mu
    var = jnp.mean(zc * zc, axis=1, keepdims=True)
    return zc * lax.rsqrt(var + NORM_EPS) * ln_g + ln_b


def _outproj_kernel(ya_ref, ob_ref, x_ref, w_ref, mod_ref, lng_ref, lnb_ref, *rest, alpha, routed):
    if routed:
        wr_ref, xo_ref, h_ref, lg_ref = rest
    else:
        xo_ref, h_ref = rest
    y = jnp.dot(ya_ref[...], w_ref[0:512, :], preferred_element_type=F32)
    y = y + jnp.dot(ob_ref[...], w_ref[512:1024, :], preferred_element_type=F32)
    xn = _resid_ln(x_ref[...], y, mod_ref[0, 2:3, :], lng_ref[...], lnb_ref[...], alpha)
    xo_ref[...] = xn
    h = xn * (1.0 + mod_ref[0, 4:5, :]) + mod_ref[0, 3:4, :]
    if routed:
        h_ref[...] = h
        lg_ref[...] = jnp.dot(h, wr_ref[...], precision=HIGHEST, preferred_element_type=F32)
    else:
        h_ref[...] = h.astype(BF16)


def _outproj(ya, ob, x2d, w_out, mod_l, ln_g, ln_b, w_router_p, seq, tm, alpha):
    m, d = x2d.shape
    tpb = seq // tm
    routed = w_router_p is not None
    kern = functools.partial(_outproj_kernel, alpha=alpha, routed=routed)
    in_specs = [
        pl.BlockSpec((tm, 512), lambda i: (i, 0)),
        pl.BlockSpec((tm, 512), lambda i: (i, 0)),
        pl.BlockSpec((tm, d), lambda i: (i, 0)),
        pl.BlockSpec((d, d), lambda i: (0, 0)),
        pl.BlockSpec((1, 6, d), lambda i: (i // tpb, 0, 0)),
        pl.BlockSpec((1, d), lambda i: (0, 0)),
        pl.BlockSpec((1, d), lambda i: (0, 0)),
    ]
    args = [ya, ob, x2d, w_out, mod_l, ln_g.reshape(1, d), ln_b.reshape(1, d)]
    out_specs = [pl.BlockSpec((tm, d), lambda i: (i, 0)), pl.BlockSpec((tm, d), lambda i: (i, 0))]
    out_shape = [jax.ShapeDtypeStruct((m, d), F32), jax.ShapeDtypeStruct((m, d), F32 if routed else BF16)]
    if routed:
        in_specs.append(pl.BlockSpec((d, 128), lambda i: (0, 0)))
        args.append(w_router_p)
        out_specs.append(pl.BlockSpec((tm, 128), lambda i: (i, 0)))
        out_shape.append(jax.ShapeDtypeStruct((m, 128), F32))
    return pl.pallas_call(
        kern,
        grid=(m // tm,),
        in_specs=in_specs,
        out_specs=out_specs,
        out_shape=out_shape,
        compiler_params=_cparams(("arbitrary",)),
    )(*args)


def _swiglu_acc(h, wg_ref, wu_ref, wd_ref, acc_ref, tf, widx):
    c0 = 0
    while c0 < tf:
        c1 = min(c0 + 512, tf)
        if widx is None:
            wg, wu, wd = wg_ref[:, c0:c1], wu_ref[:, c0:c1], wd_ref[c0:c1, :]
        else:
            wg, wu, wd = wg_ref[0, :, c0:c1], wu_ref[0, :, c0:c1], wd_ref[0, c0:c1, :]
        g = jnp.dot(h, wg, preferred_element_type=F32)
        u = jnp.dot(h, wu, preferred_element_type=F32)
        a = (g * _sigmoid(g) * u).astype(BF16)
        acc_ref[...] += jnp.dot(a, wd, preferred_element_type=F32)
        c0 = c1


def _ffn_dense_kernel(h_ref, x_ref, wg_ref, wu_ref, wd_ref, mod_ref, lng_ref, lnb_ref, xo_ref, acc_ref,
                      *, tf, alpha):
    f = pl.program_id(1)

    @pl.when(f == 0)
    def _():
        acc_ref[...] = jnp.zeros(acc_ref.shape, F32)

    _swiglu_acc(h_ref[...], wg_ref, wu_ref, wd_ref, acc_ref, tf, None)

    @pl.when(f == pl.num_programs(1) - 1)
    def _():
        xo_ref[...] = _resid_ln(x_ref[...], acc_ref[...], mod_ref[0, 5:6, :], lng_ref[...], lnb_ref[...], alpha)


def _ffn_dense(h, x2d, wg, wu, wd, mod_l, ln_g, ln_b, seq, tm, tf, alpha):
    m, d = x2d.shape
    dff = wg.shape[1]
    tpb = seq // tm
    kern = functools.partial(_ffn_dense_kernel, tf=tf, alpha=alpha)
    return pl.pallas_call(
        kern,
        grid=(m // tm, dff // tf),
        in_specs=[
            pl.BlockSpec((tm, d), lambda i, f: (i, 0)),
            pl.BlockSpec((tm, d), lambda i, f: (i, 0)),
            pl.BlockSpec((d, tf), lambda i, f: (0, f)),
            pl.BlockSpec((d, tf), lambda i, f: (0, f)),
            pl.BlockSpec((tf, d), lambda i, f: (f, 0)),
            pl.BlockSpec((1, 6, d), lambda i, f: (i // tpb, 0, 0)),
            pl.BlockSpec((1, d), lambda i, f: (0, 0)),
            pl.BlockSpec((1, d), lambda i, f: (0, 0)),
        ],
        out_specs=pl.BlockSpec((tm, d), lambda i, f: (i, 0)),
        out_shape=jax.ShapeDtypeStruct((m, d), F32),
        scratch_shapes=[pltpu.VMEM((tm, d), F32)],
        compiler_params=_cparams(("arbitrary", "arbitrary")),
    )(h, x2d, wg, wu, wd, mod_l, ln_g.reshape(1, d), ln_b.reshape(1, d))


def _ffn_grouped_kernel(te_ref, nu_ref, xs_ref, wg_ref, wu_ref, wd_ref, yo_ref, acc_ref, *, tf):
    i = pl.program_id(0)
    f = pl.program_id(1)
    used = i < nu_ref[0]

    @pl.when(jnp.logical_and(used, f == 0))
    def _():
        acc_ref[...] = jnp.zeros(acc_ref.shape, F32)

    @pl.when(used)
    def _():
        _swiglu_acc(xs_ref[...].astype(BF16), wg_ref, wu_ref, wd_ref, acc_ref, tf, 0)

    @pl.when(jnp.logical_and(used, f == pl.num_programs(1) - 1))
    def _():
        yo_ref[...] = acc_ref[...]


def _ffn_grouped(xs, tile_expert, n_used, wg, wu, wd, tm, tf):
    n_rows, d = xs.shape
    dff = wg.shape[2]
    nf = dff // tf

    def w_map(i, f, te, nu):
        return (te[i], 0, jnp.where(i < nu[0], f, nf - 1))

    def wd_map(i, f, te, nu):
        return (te[i], jnp.where(i < nu[0], f, nf - 1), 0)

    kern = functools.partial(_ffn_grouped_kernel, tf=tf)
    return pl.pallas_call(
        kern,
        grid_spec=pltpu.PrefetchScalarGridSpec(
            num_scalar_prefetch=2,
            grid=(n_rows // tm, nf),
            in_specs=[
                pl.BlockSpec((tm, d), lambda i, f, te, nu: (i, 0)),
                pl.BlockSpec((1, d, tf), w_map),
                pl.BlockSpec((1, d, tf), w_map),
                pl.BlockSpec((1, tf, d), wd_map),
            ],
            out_specs=pl.BlockSpec((tm, d), lambda i, f, te, nu: (i, 0)),
            scratch_shapes=[pltpu.VMEM((tm, d), F32)],
        ),
        out_shape=jax.ShapeDtypeStruct((n_rows, d), F32),
        compiler_params=_cparams(("arbitrary", "arbitrary")),
    )(tile_expert, n_used, xs, wg, wu, wd)


def _row_copy(src, dst, sem):
    return pltpu.make_async_copy(src, dst, sem)


def _scatter_kernel(dest_ref, h_ref, xs_ref, sem, *, ts):
    def start(r, carry):
        for s in range(2):
            _row_copy(h_ref.at[pl.ds(r, 1), :], xs_ref.at[pl.ds(dest_ref[0, 0, 2 * r + s], 1), :], sem).start()
        return carry

    lax.fori_loop(0, ts, start, 0)

    def wait(r, carry):
        for s in range(2):
            _row_copy(h_ref.at[pl.ds(0, 1), :], xs_ref.at[pl.ds(0, 1), :], sem).wait()
        return carry

    lax.fori_loop(0, ts, wait, 0)


def _scatter_rows(h, dest, n_rows, ts):
    n_tok, d = h.shape
    kern = functools.partial(_scatter_kernel, ts=ts)
    return pl.pallas_call(
        kern,
        grid=(n_tok // ts,),
        in_specs=[
            pl.BlockSpec((1, 1, 2 * ts), lambda i: (i, 0, 0), memory_space=pltpu.SMEM),
            pl.BlockSpec((ts, d), lambda i: (i, 0)),
        ],
        out_specs=pl.BlockSpec(memory_space=pl.ANY),
        out_shape=jax.ShapeDtypeStruct((n_rows, d), F32),
        scratch_shapes=[pltpu.SemaphoreType.DMA(())],
        compiler_params=_cparams(("arbitrary",)),
    )(dest.reshape(n_tok // ts, 1, 2 * ts), h)


def _combine_kernel(dest_ref, gate_ref, x_ref, mod_ref, lng_ref, lnb_ref, yb_ref, xo_ref, r0_ref, r1_ref, sem,
                    *, ts, alpha):
    def start(r, carry):
        _row_copy(yb_ref.at[pl.ds(dest_ref[0, 0, 2 * r], 1), :], r0_ref.at[pl.ds(r, 1), :], sem).start()
        _row_copy(yb_ref.at[pl.ds(dest_ref[0, 0, 2 * r + 1], 1), :], r1_ref.at[pl.ds(r, 1), :], sem).start()
        return carry

    lax.fori_loop(0, ts, start, 0)

    def wait(r, carry):
        for s in range(2):
            _row_copy(yb_ref.at[pl.ds(0, 1), :], r0_ref.at[pl.ds(0, 1), :], sem).wait()
        return carry

    lax.fori_loop(0, ts, wait, 0)

    y = gate_ref[:, 0:1] * r0_ref[...] + gate_ref[:, 1:2] * r1_ref[...]
    xo_ref[...] = _resid_ln(x_ref[...], y, mod_ref[0, 5:6, :], lng_ref[...], lnb_ref[...], alpha)


def _combine(dest, gates, x2d, mod_l, ln_g, ln_b, yb, seq, ts, alpha):
    n_tok, d = x2d.shape
    tpb = seq // ts
    kern = functools.partial(_combine_kernel, ts=ts, alpha=alpha)
    return pl.pallas_call(
        kern,
        grid=(n_tok // ts,),
        in_specs=[
            pl.BlockSpec((1, 1, 2 * ts), lambda i: (i, 0, 0), memory_space=pltpu.SMEM),
            pl.BlockSpec((ts, 2), lambda i: (i, 0)),
            pl.BlockSpec((ts, d), lambda i: (i, 0)),
            pl.BlockSpec((1, 6, d), lambda i: (i // tpb, 0, 0)),
            pl.BlockSpec((1, d), lambda i: (0, 0)),
            pl.BlockSpec((1, d), lambda i: (0, 0)),
            pl.BlockSpec(memory_space=pl.ANY),
        ],
        out_specs=pl.BlockSpec((ts, d), lambda i: (i, 0)),
        out_shape=jax.ShapeDtypeStruct((n_tok, d), F32),
        scratch_shapes=[pltpu.VMEM((ts, d), F32), pltpu.VMEM((ts, d), F32), pltpu.SemaphoreType.DMA(())],
        compiler_params=_cparams(("arbitrary",)),
    )(dest.reshape(n_tok // ts, 1, 2 * ts), gates, x2d, mod_l, ln_g.reshape(1, d), ln_b.reshape(1, d), yb)


def _route(logits, tm_e):
    n_tok = logits.shape[0]
    top_val, top_idx = lax.top_k(logits, 2)
    gates = jax.nn.softmax(top_val, axis=-1)
    e_flat = top_idx.reshape(-1).astype(I32)
    onehot = (e_flat[:, None] == jnp.arange(N_EXPERTS, dtype=I32)[None, :]).astype(I32)
    csum = jnp.cumsum(onehot, axis=0)
    counts = csum[-1]
    rank = jnp.sum(csum * onehot, axis=1) - 1
    padded = ((counts + tm_e - 1) // tm_e) * tm_e
    pend = jnp.cumsum(padded)
    pstart = pend - padded
    dest = (pstart[e_flat] + rank).astype(I32)
    n_tiles = (2 * n_tok) // tm_e + N_EXPERTS
    n_used = (pend[-1] // tm_e).astype(I32)
    tile_start = jnp.arange(n_tiles, dtype=I32) * tm_e
    tile_expert = jnp.minimum(jnp.searchsorted(pend, tile_start, side='right'), N_EXPERTS - 1).astype(I32)
    last_e = tile_expert[jnp.maximum(n_used - 1, 0)]
    tile_expert = jnp.where(jnp.arange(n_tiles) < n_used, tile_expert, last_e)
    return dest, gates, tile_expert, n_used.reshape(1), n_tiles


def _reorder_w_in(w_in_l):
    widths = (512, 64, 64, 512, 64, 8, 256, 256, 512, 512, 16)
    offs = [0]
    for wd_ in widths:
        offs.append(offs[-1] + wd_)
    seg = [w_in_l[:, offs[k]:offs[k + 1]] for k in range(len(widths))]
    aq, ak, av, iq, ik, iw, bq, bk, bv, bg, ba = seg
    d = w_in_l.shape[0]
    z = lambda n: jnp.zeros((d, n), w_in_l.dtype)
    aq = aq * (1.4426950408889634 * A_HEAD_DIM ** -0.5)
    w_r = jnp.concatenate([aq, iq, ak, z(64), ik, z(64), bq, bk, bv, bg, iw, ba, z(104)], axis=1).astype(BF16)
    return w_r, jnp.transpose(av).astype(BF16)


def kernel(x, c, w_in, w_a2, b_a, gla_norm_g, w_out, w_mod, b_mod, ln_g, ln_b, ffn_w_gate, ffn_w_up, ffn_w_down,
           w_router, b_router, moe_w_gate, moe_w_up, moe_w_down):
    bsz, seq, d = x.shape
    depth = w_in.shape[0]
    dff = ffn_w_gate.shape[2]
    m = bsz * seq
    alpha = (2 * depth) ** 0.25

    tm_p = min(512, seq)
    tm_f = min(1024, seq)
    tm_e = min(1024, seq)
    T = min(128, seq)
    tk = min(512, seq)
    R = min(512, seq)
    ts = min(256, seq)
    tf = dff // 2 if (dff // 2) % 128 == 0 else dff

    mod = _modulation(c, w_mod, b_mod).reshape(depth, bsz, 6, d)
    x2d = x.reshape(m, d)
    for l in range(depth):
        mod_l = mod[l]
        w_r, w_vt = _reorder_w_in(w_in[l])
        p1, p2, p3, vt = _proj(x2d, mod_l, w_r, w_vt, seq, tm_p, tk)
        ya = _dsa(p1, p3, vt, bsz, seq, T, tk)
        ob = _gla(p2, p3, w_a2[l], b_a[l], gla_norm_g[l], bsz, seq, R)
        j = l // 2
        if l % 2 == 0:
            x1, h2 = _outproj(ya, ob, x2d, w_out[l].astype(BF16), mod_l, ln_g[l, 0], ln_b[l, 0], None,
                              seq, tm_p, alpha)
            x2d = _ffn_dense(h2, x1, ffn_w_gate[j].astype(BF16), ffn_w_up[j].astype(BF16),
                             ffn_w_down[j].astype(BF16), mod_l, ln_g[l, 1], ln_b[l, 1], seq, tm_f, tf, alpha)
        else:
            wr = jnp.zeros((d, 128), F32).at[:, :N_EXPERTS].set(w_router[j])
            x1, h2, lg = _outproj(ya, ob, x2d, w_out[l].astype(BF16), mod_l, ln_g[l, 0], ln_b[l, 0], wr,
                                  seq, tm_p, alpha)
            logits = lg[:, :N_EXPERTS] + b_router[j][None, :]
            dest, gates, tile_expert, n_used, n_tiles = _route(logits, tm_e)
            xs = _scatter_rows(h2, dest, n_tiles * tm_e, ts)
            yb = _ffn_grouped(xs, tile_expert, n_used, moe_w_gate[j].astype(BF16), moe_w_up[j].astype(BF16),
                              moe_w_down[j].astype(BF16), tm_e, tf)
            x2d = _combine(dest, gates, x1, mod_l, ln_g[l, 1], ln_b[l, 1], yb, seq, ts, alpha)
    return x2d.reshape(bsz, seq, d)
```

```python
import functools

import jax
import jax.numpy as jnp
from jax import lax
from jax.experimental import pallas as pl
from jax.experimental.pallas import tpu as pltpu

F32 = jnp.float32
BF16 = jnp.bfloat16
I32 = jnp.int32
HIGHEST = lax.Precision.HIGHEST

A_HEADS = 8
A_HEAD_DIM = 64
IDX_HEADS = 8
IDX_DIM = 64
TOPK_MAX = 256
B_HEADS = 4
B_KEY_DIM = 64
B_VAL_DIM = 128
GATE_RANK = 16
GATE_TAU = 16.0
GLA_CHUNK = 64
N_EXPERTS = 8
NORM_EPS = 1e-5

N1, N2, N3 = 1280, 1024, 640
IW_OFF = 0
BA_OFF = 8

ONES_ROWS = 16
KEY_CHUNK = 128
LANE_GROUP = 256
VMEM_LIMIT = 56 * 1024 * 1024
INT_MIN = -2147483648
NEG_BIG = -1e30


def _sigmoid(x):
    return 1.0 / (1.0 + jnp.exp(-x))


def _cparams(sem):
    return pltpu.CompilerParams(dimension_semantics=sem, vmem_limit_bytes=VMEM_LIMIT)


def _mod_kernel(c_ref, w_ref, b_ref, o_ref):
    c = c_ref[...]
    ca = c * _sigmoid(c)
    o_ref[0] = jnp.dot(ca, w_ref[0], precision=HIGHEST, preferred_element_type=F32) + b_ref[0]


def _modulation(c, w_mod, b_mod):
    depth, d, n = w_mod.shape
    bsz = c.shape[0]
    tn = 1536 if n % 1536 == 0 else n
    return pl.pallas_call(
        _mod_kernel,
        grid=(depth, n // tn),
        in_specs=[
            pl.BlockSpec((bsz, d), lambda l, j: (0, 0)),
            pl.BlockSpec((1, d, tn), lambda l, j: (l, 0, j)),
            pl.BlockSpec((1, 1, tn), lambda l, j: (l, 0, j)),
        ],
        out_specs=pl.BlockSpec((1, bsz, tn), lambda l, j: (l, 0, j)),
        out_shape=jax.ShapeDtypeStruct((depth, bsz, n), F32),
        compiler_params=_cparams(("arbitrary", "arbitrary")),
    )(c, w_mod, b_mod.reshape(depth, 1, n))


def _proj_kernel(x_ref, mod_ref, w_ref, wvt_ref, o1_ref, o2_ref, o3_ref, vt_ref, *, tk):
    x = x_ref[...]
    sh = mod_ref[0, 0:1, :]
    sc = mod_ref[0, 1:2, :]
    h = (x * (1.0 + sc) + sh).astype(BF16)
    o1_ref[...] = jnp.dot(h, w_ref[:, 0:N1], preferred_element_type=F32).astype(BF16)
    o2_ref[...] = jnp.dot(h, w_ref[:, N1:N1 + N2], preferred_element_type=F32).astype(BF16)
    o3_ref[...] = jnp.dot(h, w_ref[:, N1 + N2:N1 + N2 + N3], preferred_element_type=F32)
    nt = (((1,), (1,)), ((), ()))
    for c in range(vt_ref.shape[0]):
        vt_ref[c] = lax.dot_general(wvt_ref[...], h[c * tk:(c + 1) * tk, :], nt,
                                    preferred_element_type=F32).astype(BF16)


def _proj(x2d, mod_l, w_in_r, w_vt, seq, tm, tk):
    m, d = x2d.shape
    tpb = seq // tm
    nw = N1 + N2 + N3
    kern = functools.partial(_proj_kernel, tk=tk)
    return pl.pallas_call(
        kern,
        grid=(m // tm,),
        in_specs=[
            pl.BlockSpec((tm, d), lambda i: (i, 0)),
            pl.BlockSpec((1, 6, d), lambda i: (i // tpb, 0, 0)),
            pl.BlockSpec((d, nw), lambda i: (0, 0)),
            pl.BlockSpec((A_HEAD_DIM, d), lambda i: (0, 0)),
        ],
        out_specs=[
            pl.BlockSpec((tm, N1), lambda i: (i, 0)),
            pl.BlockSpec((tm, N2), lambda i: (i, 0)),
            pl.BlockSpec((tm, N3), lambda i: (i, 0)),
            pl.BlockSpec((tm // tk, A_HEAD_DIM, tk), lambda i: (i, 0, 0)),
        ],
        out_shape=[
            jax.ShapeDtypeStruct((m, N1), BF16),
            jax.ShapeDtypeStruct((m, N2), BF16),
            jax.ShapeDtypeStruct((m, N3), F32),
            jax.ShapeDtypeStruct((m // tk, A_HEAD_DIM, tk), BF16),
        ],
        compiler_params=_cparams(("arbitrary",)),
    )(x2d, mod_l, w_in_r, w_vt)


def _dsa_kernel(q_ref, kv_ref, vt_ref, w_ref, qext_ref, postab_ref, o_ref,
                kx_ref, vtx_ref, qx_ref, qis_ref, keys_ref, sa_ref, sb_ref, pa_ref, pb_ref, ala_ref, alb_ref,
                m_ref, acc_ref, *, T, tk, k_top):
    i = pl.program_id(1)
    t0 = i * T
    log_tk = tk.bit_length() - 1
    nk = lax.shift_right_logical(t0 + T + tk - 1, log_tk)
    G = LANE_GROUP // T
    NG = A_HEADS // G
    KC = KEY_CHUNK
    NC = tk // KC
    nt = (((1,), (1,)), ((), ()))

    @pl.when(i == 0)
    def _():
        kx_ref[...] = kv_ref[:, 0:128] + postab_ref[...]
        qx_ref[:, 64:128] = qext_ref[...]
        vtx_ref[:, 0:A_HEAD_DIM, :] = vt_ref[...]
        vtx_ref[:, A_HEAD_DIM:, :] = jnp.ones((vtx_ref.shape[0], ONES_ROWS, tk), BF16)

    for h in range(A_HEADS):
        qx_ref[h * T:(h + 1) * T, 0:64] = q_ref[:, h * A_HEAD_DIM:(h + 1) * A_HEAD_DIM]
        qis_ref[h * T:(h + 1) * T, :] = q_ref[:, 512 + h * IDX_DIM:512 + (h + 1) * IDX_DIM]

    row = lax.broadcasted_iota(I32, (tk, T), 0)
    col = lax.broadcasted_iota(I32, (tk, T), 1)
    cmr = col - row

    w_t = jnp.transpose(w_ref[...]) * ((IDX_HEADS ** -0.5) * (IDX_DIM ** -0.5))
    w_rows = [w_t[IW_OFF + h:IW_OFF + h + 1, :] for h in range(IDX_HEADS)]

    def p1_body(j, carry):
        off = pl.multiple_of(j * tk, tk)
        ki = kv_ref[pl.ds(off, tk), 128:192]
        acc = jnp.zeros((tk, T), F32)
        for g in range(NG):
            s = lax.dot_general(ki, qis_ref[g * LANE_GROUP:(g + 1) * LANE_GROUP, :], nt,
                                preferred_element_type=F32)
            for hh in range(G):
                acc = acc + w_rows[g * G + hh] * jnp.maximum(s[:, hh * T:(hh + 1) * T], 0.0)
        bits = pltpu.bitcast(acc, I32)
        key = jnp.where(bits >= 0, bits, bits ^ 0x7FFFFFFF)
        valid = (cmr + (t0 - j * tk)) >= 0
        keys_ref[j] = jnp.where(valid, key, INT_MIN).reshape(tk // 8, 8, T)
        return carry

    lax.fori_loop(0, nk, p1_body, 0)

    kf = float(k_top)

    def count_ge(cand_s):
        cand_b = jnp.broadcast_to(cand_s, (8, T))[None]

        def cnt_body(j, c):
            x = jnp.where(keys_ref[j] >= cand_b, 1.0, 0.0)
            return c + jnp.sum(x.reshape(4, tk // 32, 8, T), axis=1)

        c = lax.fori_loop(0, nk, cnt_body, jnp.zeros((4, 8, T), F32))
        return jnp.sum(jnp.sum(c, axis=0), axis=0, keepdims=True)

    def bit_body(bi, carry):
        mu, cnt_mu = carry
        cand_u = mu | lax.shift_left(jnp.int32(1), 31 - bi)
        cnt = count_ge(cand_u ^ INT_MIN)
        ok = cnt >= kf
        return jnp.where(ok, cand_u, mu), jnp.where(ok, cnt, cnt_mu)

    mu, cnt_ge = lax.fori_loop(0, 32, bit_body, (jnp.zeros((1, T), I32), jnp.zeros((1, T), F32)))
    thr = jnp.maximum(mu ^ INT_MIN, INT_MIN + 1)

    surplus = jnp.where(mu != 0, cnt_ge - kf, 0.0)
    has_ties = jnp.max(surplus) > 0.0

    @pl.when(has_ties)
    def _():
        need = kf - count_ge(thr + 1)
        thr_b = jnp.broadcast_to(thr, (tk, T))
        r2 = lax.broadcasted_iota(I32, (tk, tk), 0)
        c2 = lax.broadcasted_iota(I32, (tk, tk), 1)
        tri = jnp.where(r2 >= c2, 1.0, 0.0).astype(BF16)

        def fix_body(j, base):
            tile = keys_ref[j].reshape(tk, T)
            tie = tile == thr_b
            pref = jnp.dot(tri, jnp.where(tie, 1.0, 0.0).astype(BF16), preferred_element_type=F32) + base
            drop = jnp.logical_and(tie, pref > need)
            keys_ref[j] = jnp.where(drop, tile - 1, tile).reshape(tk // 8, 8, T)
            return pref[tk - 1:tk, :]

        lax.fori_loop(0, nk, fix_body, jnp.zeros((1, T), F32))

    nkt = keys_ref.shape[0]
    s_bufs = (sa_ref, sb_ref)
    p_bufs = (pa_ref, pb_ref)

    def groups_of_step(c):
        return range(c * NG // NC, (c + 1) * NG // NC)

    al_bufs = (ala_ref, alb_ref)
    m_ref[...] = jnp.full(m_ref.shape, NEG_BIG, F32)
    acc_ref[...] = jnp.zeros(acc_ref.shape, F32)
    pb_ref[...] = jnp.zeros(pb_ref.shape, BF16)
    alb_ref[...] = jnp.ones(alb_ref.shape, F32)
    thr_b = jnp.broadcast_to(thr, (KC, T))

    def qk_matmul(j, s_ref, groups):
        off = pl.multiple_of(jnp.minimum(j, nkt - 1) * tk, tk)
        kx = kx_ref[pl.ds(off, tk), :]
        for g in groups:
            s_ref[g] = lax.dot_general(kx, qx_ref[g * LANE_GROUP:(g + 1) * LANE_GROUP, :], nt,
                                       preferred_element_type=F32)

    def softmax(j, s_ref, p_ref, al_ref, c):
        rows = slice(c * KC, (c + 1) * KC)
        sel = keys_ref[j, c * (KC // 8):(c + 1) * (KC // 8)].reshape(KC, T) >= thr_b
        sel_g = jnp.concatenate([sel] * G, axis=1) if G > 1 else sel
        for g in range(NG):
            s = jnp.where(sel_g, s_ref[g, rows, :], NEG_BIG)
            m_old = m_ref[g]
            m_new = jnp.maximum(m_old, jnp.max(s, axis=0, keepdims=True))
            al_ref[g * NC + c] = jnp.exp2(m_old - m_new)
            p_ref[g, rows, :] = jnp.exp2(s - m_new).astype(BF16)
            m_ref[g] = m_new

    def pv_update(j, p_ref, al_ref, c):
        vtx = vtx_ref[jnp.maximum(j, 0), :, c * KC:(c + 1) * KC]
        for g in range(NG):
            acc_ref[g] = al_ref[g * NC + c] * acc_ref[g] + jnp.dot(vtx, p_ref[g, c * KC:(c + 1) * KC, :],
                                                                   preferred_element_type=F32)

    qk_matmul(0, s_bufs[0], range(NG))

    def p3_pair(jj, carry):
        for par in range(2):
            j = 2 * jj + par

            @pl.when(j < nk)
            def _():
                for c in range(NC):
                    pv_update(j - 1, p_bufs[1 - par], al_bufs[1 - par], c)
                    qk_matmul(j + 1, s_bufs[1 - par], groups_of_step(c))
                    softmax(j, s_bufs[par], p_bufs[par], al_bufs[par], c)
        return carry

    lax.fori_loop(0, lax.shift_right_logical(nk + 1, 1), p3_pair, 0)

    for par in range(2):
        @pl.when(((nk - 1) & 1) == par)
        def _():
            for c in range(NC):
                pv_update(nk - 1, p_bufs[par], al_bufs[par], c)

    def head_out(h):
        g, hh = h // G, h % G
        a = acc_ref[g]
        return a[0:A_HEAD_DIM, hh * T:(hh + 1) * T] / a[A_HEAD_DIM:A_HEAD_DIM + 1, hh * T:(hh + 1) * T]

    for h in range(0, A_HEADS, 2):
        pair = jnp.concatenate([head_out(h), head_out(h + 1)], axis=0)
        o_ref[:, h * A_HEAD_DIM:(h + 2) * A_HEAD_DIM] = jnp.transpose(pair).astype(BF16)


def _alibi_tables(seq, T):
    log2e = 1.4426950408889634
    pos = jnp.arange(seq, dtype=I32)
    hi = (pos // 64).astype(F32)
    lo = (pos % 64).astype(F32)
    z = jnp.zeros((seq,), F32)
    pos_cols = [z] * 64 + [hi, hi, hi, lo, lo, lo] + [z] * 58
    postab = jnp.stack(pos_cols, axis=1).astype(BF16)
    rows = []
    for h in range(A_HEADS):
        c = jnp.asarray(2.0 ** (-8.0 * (h + 1) / A_HEADS) * log2e, F32)
        c1 = c.astype(BF16)
        c2 = (c - c1.astype(F32)).astype(BF16)
        c3 = (c - c1.astype(F32) - c2.astype(F32)).astype(BF16)
        r = jnp.zeros((64,), BF16)
        r = r.at[0].set(c1 * 64).at[1].set(c2 * 64).at[2].set(c3 * 64).at[3].set(c1).at[4].set(c2).at[5].set(c3)
        rows.append(jnp.broadcast_to(r[None, :], (T, 64)))
    qext = jnp.concatenate(rows, axis=0)
    return postab, qext


def _dsa(p1, p3, vt, bsz, seq, T, tk):
    m = p1.shape[0]
    nq = seq // T
    nkt = seq // tk
    k_top = min(TOPK_MAX, seq // 4)
    postab, qext = _alibi_tables(seq, T)
    kern = functools.partial(_dsa_kernel, T=T, tk=tk, k_top=k_top)
    ng = A_HEADS * T // LANE_GROUP
    return pl.pallas_call(
        kern,
        grid=(bsz, nq),
        in_specs=[
            pl.BlockSpec((T, 1024), lambda b, i: (b * nq + i, 0)),
            pl.BlockSpec((seq, 256), lambda b, i: (b, 4)),
            pl.BlockSpec((nkt, A_HEAD_DIM, tk), lambda b, i: (b, 0, 0)),
            pl.BlockSpec((T, 128), lambda b, i: (b * nq + i, 4)),
            pl.BlockSpec((A_HEADS * T, 64), lambda b, i: (0, 0)),
            pl.BlockSpec((seq, 128), lambda b, i: (0, 0)),
        ],
        out_specs=pl.BlockSpec((T, 512), lambda b, i: (b * nq + i, 0)),
        out_shape=jax.ShapeDtypeStruct((m, 512), BF16),
        scratch_shapes=[
            pltpu.VMEM((seq, 128), BF16),
            pltpu.VMEM((nkt, A_HEAD_DIM + ONES_ROWS, tk), BF16),
            pltpu.VMEM((A_HEADS * T, 128), BF16),
            pltpu.VMEM((A_HEADS * T, IDX_DIM), BF16),
            pltpu.VMEM((nkt, tk // 8, 8, T), I32),
            pltpu.VMEM((ng, tk, LANE_GROUP), F32),
            pltpu.VMEM((ng, tk, LANE_GROUP), F32),
            pltpu.VMEM((ng, tk, LANE_GROUP), BF16),
            pltpu.VMEM((ng, tk, LANE_GROUP), BF16),
            pltpu.VMEM((ng * (tk // KEY_CHUNK), 1, LANE_GROUP), F32),
            pltpu.VMEM((ng * (tk // KEY_CHUNK), 1, LANE_GROUP), F32),
            pltpu.VMEM((ng, 1, LANE_GROUP), F32),
            pltpu.VMEM((ng, A_HEAD_DIM + ONES_ROWS, LANE_GROUP), F32),
        ],
        compiler_params=_cparams(("arbitrary", "arbitrary")),
    )(p1, p1, vt, p3, qext, postab)


def _gla_kernel(p2_ref, p3_ref, wa2_ref, ba_ref, g_ref, o_ref, st_ref, *, R):
    C = GLA_CHUNK

    @pl.when(pl.program_id(1) == 0)
    def _():
        st_ref[...] = jnp.zeros(st_ref.shape, F32)

    ba = p3_ref[:, 512 + BA_OFF:512 + BA_OFF + GATE_RANK]
    xg = jnp.dot(ba, wa2_ref[...], precision=HIGHEST, preferred_element_type=F32) + ba_ref[...]
    log_a = (jnp.minimum(xg, 0.0) - jnp.log(1.0 + jnp.exp(-jnp.abs(xg)))) * (1.0 / GATE_TAU)

    r_i = lax.broadcasted_iota(I32, (C, C), 0)
    c_i = lax.broadcasted_iota(I32, (C, C), 1)
    tri_b = r_i >= c_i
    tri_f = tri_b.astype(F32)
    nt = (((1,), (1,)), ((), ()))
    tn = (((0,), (0,)), ((), ()))
    gamma = g_ref[...]

    for c in range(R // C):
        rows = slice(c * C, (c + 1) * C)
        g = log_a[rows, :]
        b = jnp.dot(tri_f, g, precision=HIGHEST, preferred_element_type=F32)
        b_last = b[C - 1:C, :]
        b_mid = b[C // 2 - 1:C // 2, :]
        q = p2_ref[rows, 0:256].astype(F32) * (B_KEY_DIM ** -0.5)
        k = p2_ref[rows, 256:512].astype(F32)
        qe = (q * jnp.exp(b)).astype(BF16)
        qm = (q * jnp.exp(b - b_mid)).astype(BF16)
        km = (k * jnp.exp(b_mid - b)).astype(BF16)
        kl = (k * jnp.exp(b_last - b)).astype(BF16)
        dec = jnp.exp(b_last)
        for h in range(B_HEADS):
            ks = slice(h * B_KEY_DIM, (h + 1) * B_KEY_DIM)
            vs = slice(512 + h * B_VAL_DIM, 512 + (h + 1) * B_VAL_DIM)
            v = p2_ref[rows, vs]
            st = st_ref[h]
            a = lax.dot_general(qm[:, ks], km[:, ks], nt, preferred_element_type=F32)
            a = jnp.where(tri_b, a, 0.0).astype(BF16)
            o = lax.dot_general(qe[:, ks], st.astype(BF16), nt, preferred_element_type=F32)
            o = o + jnp.dot(a, v, preferred_element_type=F32)
            st_ref[h] = st * dec[:, ks] + lax.dot_general(v, kl[:, ks], tn, preferred_element_type=F32)
            ms = jnp.mean(o * o, axis=1, keepdims=True)
            on = o * lax.rsqrt(ms + NORM_EPS) * gamma
            gate = p3_ref[rows, h * B_VAL_DIM:(h + 1) * B_VAL_DIM]
            o_ref[rows, h * B_VAL_DIM:(h + 1) * B_VAL_DIM] = (on * (gate * _sigmoid(gate))).astype(BF16)


def _gla(p2, p3, w_a2, b_a, gamma, bsz, seq, R):
    m = p2.shape[0]
    nr = seq // R
    kern = functools.partial(_gla_kernel, R=R)
    return pl.pallas_call(
        kern,
        grid=(bsz, nr),
        in_specs=[
            pl.BlockSpec((R, N2), lambda b, i: (b * nr + i, 0)),
            pl.BlockSpec((R, N3), lambda b, i: (b * nr + i, 0)),
            pl.BlockSpec((GATE_RANK, 256), lambda b, i: (0, 0)),
            pl.BlockSpec((1, 256), lambda b, i: (0, 0)),
            pl.BlockSpec((1, B_VAL_DIM), lambda b, i: (0, 0)),
        ],
        out_specs=pl.BlockSpec((R, 512), lambda b, i: (b * nr + i, 0)),
        out_shape=jax.ShapeDtypeStruct((m, 512), BF16),
        scratch_shapes=[pltpu.VMEM((B_HEADS, B_VAL_DIM, B_KEY_DIM), F32)],
        compiler_params=_cparams(("arbitrary", "arbitrary")),
    )(p2, p3, w_a2, b_a.reshape(1, 256), gamma.reshape(1, B_VAL_DIM))


def _resid_ln(x, y, gate, ln_g, ln_b, alpha):
    z = alpha * x + (1.0 + gate) * y
    mu = jnp.mean(z, axis=1, keepdims=True)
    zc = z - mu
    var = jnp.mean(zc * zc, axis=1, keepdims=True)
    return zc * lax.rsqrt(var + NORM_EPS) * ln_g + ln_b


def _outproj_kernel(ya_ref, ob_ref, x_ref, w_ref, mod_ref, lng_ref, lnb_ref, *rest, alpha, routed):
    if routed:
        wr_ref, xo_ref, h_ref, lg_ref = rest
    else:
        xo_ref, h_ref = rest
    y = jnp.dot(ya_ref[...], w_ref[0:512, :], preferred_element_type=F32)
    y = y + jnp.dot(ob_ref[...], w_ref[512:1024, :], preferred_element_type=F32)
    xn = _resid_ln(x_ref[...], y, mod_ref[0, 2:3, :], lng_ref[...], lnb_ref[...], alpha)
    xo_ref[...] = xn
    h = xn * (1.0 + mod_ref[0, 4:5, :]) + mod_ref[0, 3:4, :]
    if routed:
        h_ref[...] = h
        lg_ref[...] = jnp.dot(h, wr_ref[...], precision=HIGHEST, preferred_element_type=F32)
    else:
        h_ref[...] = h.astype(BF16)


def _outproj(ya, ob, x2d, w_out, mod_l, ln_g, ln_b, w_router_p, seq, tm, alpha):
    m, d = x2d.shape
    tpb = seq // tm
    routed = w_router_p is not None
    kern = functools.partial(_outproj_kernel, alpha=alpha, routed=routed)
    in_specs = [
        pl.BlockSpec((tm, 512), lambda i: (i, 0)),
        pl.BlockSpec((tm, 512), lambda i: (i, 0)),
        pl.BlockSpec((tm, d), lambda i: (i, 0)),
        pl.BlockSpec((d, d), lambda i: (0, 0)),
        pl.BlockSpec((1, 6, d), lambda i: (i // tpb, 0, 0)),
        pl.BlockSpec((1, d), lambda i: (0, 0)),
        pl.BlockSpec((1, d), lambda i: (0, 0)),
    ]
    args = [ya, ob, x2d, w_out, mod_l, ln_g.reshape(1, d), ln_b.reshape(1, d)]
    out_specs = [pl.BlockSpec((tm, d), lambda i: (i, 0)), pl.BlockSpec((tm, d), lambda i: (i, 0))]
    out_shape = [jax.ShapeDtypeStruct((m, d), F32), jax.ShapeDtypeStruct((m, d), F32 if routed else BF16)]
    if routed:
        in_specs.append(pl.BlockSpec((d, 128), lambda i: (0, 0)))
        args.append(w_router_p)
        out_specs.append(pl.BlockSpec((tm, 128), lambda i: (i, 0)))
        out_shape.append(jax.ShapeDtypeStruct((m, 128), F32))
    return pl.pallas_call(
        kern,
        grid=(m // tm,),
        in_specs=in_specs,
        out_specs=out_specs,
        out_shape=out_shape,
        compiler_params=_cparams(("arbitrary",)),
    )(*args)


def _swiglu_acc(h, wg_ref, wu_ref, wd_ref, acc_ref, tf, widx):
    c0 = 0
    while c0 < tf:
        c1 = min(c0 + 512, tf)
        if widx is None:
            wg, wu, wd = wg_ref[:, c0:c1], wu_ref[:, c0:c1], wd_ref[c0:c1, :]
        else:
            wg, wu, wd = wg_ref[0, :, c0:c1], wu_ref[0, :, c0:c1], wd_ref[0, c0:c1, :]
        g = jnp.dot(h, wg, preferred_element_type=F32)
        u = jnp.dot(h, wu, preferred_element_type=F32)
        a = (g * _sigmoid(g) * u).astype(BF16)
        acc_ref[...] += jnp.dot(a, wd, preferred_element_type=F32)
        c0 = c1


def _ffn_dense_kernel(h_ref, x_ref, wg_ref, wu_ref, wd_ref, mod_ref, lng_ref, lnb_ref, xo_ref, acc_ref,
                      *, tf, alpha):
    f = pl.program_id(1)

    @pl.when(f == 0)
    def _():
        acc_ref[...] = jnp.zeros(acc_ref.shape, F32)

    _swiglu_acc(h_ref[...], wg_ref, wu_ref, wd_ref, acc_ref, tf, None)

    @pl.when(f == pl.num_programs(1) - 1)
    def _():
        xo_ref[...] = _resid_ln(x_ref[...], acc_ref[...], mod_ref[0, 5:6, :], lng_ref[...], lnb_ref[...], alpha)


def _ffn_dense(h, x2d, wg, wu, wd, mod_l, ln_g, ln_b, seq, tm, tf, alpha):
    m, d = x2d.shape
    dff = wg.shape[1]
    tpb = seq // tm
    kern = functools.partial(_ffn_dense_kernel, tf=tf, alpha=alpha)
    return pl.pallas_call(
        kern,
        grid=(m // tm, dff // tf),
        in_specs=[
            pl.BlockSpec((tm, d), lambda i, f: (i, 0)),
            pl.BlockSpec((tm, d), lambda i, f: (i, 0)),
            pl.BlockSpec((d, tf), lambda i, f: (0, f)),
            pl.BlockSpec((d, tf), lambda i, f: (0, f)),
            pl.BlockSpec((tf, d), lambda i, f: (f, 0)),
            pl.BlockSpec((1, 6, d), lambda i, f: (i // tpb, 0, 0)),
            pl.BlockSpec((1, d), lambda i, f: (0, 0)),
            pl.BlockSpec((1, d), lambda i, f: (0, 0)),
        ],
        out_specs=pl.BlockSpec((tm, d), lambda i, f: (i, 0)),
        out_shape=jax.ShapeDtypeStruct((m, d), F32),
        scratch_shapes=[pltpu.VMEM((tm, d), F32)],
        compiler_params=_cparams(("arbitrary", "arbitrary")),
    )(h, x2d, wg, wu, wd, mod_l, ln_g.reshape(1, d), ln_b.reshape(1, d))


def _ffn_grouped_kernel(te_ref, nu_ref, xs_ref, wg_ref, wu_ref, wd_ref, yo_ref, acc_ref, *, tf):
    i = pl.program_id(0)
    f = pl.program_id(1)
    used = i < nu_ref[0]

    @pl.when(jnp.logical_and(used, f == 0))
    def _():
        acc_ref[...] = jnp.zeros(acc_ref.shape, F32)

    @pl.when(used)
    def _():
        _swiglu_acc(xs_ref[...].astype(BF16), wg_ref, wu_ref, wd_ref, acc_ref, tf, 0)

    @pl.when(jnp.logical_and(used, f == pl.num_programs(1) - 1))
    def _():
        yo_ref[...] = acc_ref[...]


def _ffn_grouped(xs, tile_expert, n_used, wg, wu, wd, tm, tf):
    n_rows, d = xs.shape
    dff = wg.shape[2]
    nf = dff // tf

    def w_map(i, f, te, nu):
        return (te[i], 0, jnp.where(i < nu[0], f, nf - 1))

    def wd_map(i, f, te, nu):
        return (te[i], jnp.where(i < nu[0], f, nf - 1), 0)

    kern = functools.partial(_ffn_grouped_kernel, tf=tf)
    return pl.pallas_call(
        kern,
        grid_spec=pltpu.PrefetchScalarGridSpec(
            num_scalar_prefetch=2,
            grid=(n_rows // tm, nf),
            in_specs=[
                pl.BlockSpec((tm, d), lambda i, f, te, nu: (i, 0)),
                pl.BlockSpec((1, d, tf), w_map),
                pl.BlockSpec((1, d, tf), w_map),
                pl.BlockSpec((1, tf, d), wd_map),
            ],
            out_specs=pl.BlockSpec((tm, d), lambda i, f, te, nu: (i, 0)),
            scratch_shapes=[pltpu.VMEM((tm, d), F32)],
        ),
        out_shape=jax.ShapeDtypeStruct((n_rows, d), F32),
        compiler_params=_cparams(("arbitrary", "arbitrary")),
    )(tile_expert, n_used, xs, wg, wu, wd)


def _row_copy(src, dst, sem):
    return pltpu.make_async_copy(src, dst, sem)


ROW_UNROLL = 8


def _scatter_kernel(dest_ref, h_ref, xs_ref, sem, *, ts):
    def start(rb, carry):
        for u in range(ROW_UNROLL):
            r = rb * ROW_UNROLL + u
            for s in range(2):
                _row_copy(h_ref.at[pl.ds(r, 1), :], xs_ref.at[pl.ds(dest_ref[0, 0, 2 * r + s], 1), :],
                          sem).start(priority=s)
        return carry

    lax.fori_loop(0, ts // ROW_UNROLL, start, 0)
    for s in range(2):
        _row_copy(h_ref, xs_ref.at[pl.ds(0, ts), :], sem).wait()


def _scatter_rows(h, dest, n_rows, ts):
    n_tok, d = h.shape
    kern = functools.partial(_scatter_kernel, ts=ts)
    return pl.pallas_call(
        kern,
        grid=(n_tok // ts,),
        in_specs=[
            pl.BlockSpec((1, 1, 2 * ts), lambda i: (i, 0, 0), memory_space=pltpu.SMEM),
            pl.BlockSpec((ts, d), lambda i: (i, 0)),
        ],
        out_specs=pl.BlockSpec(memory_space=pl.ANY),
        out_shape=jax.ShapeDtypeStruct((n_rows, d), F32),
        scratch_shapes=[pltpu.SemaphoreType.DMA(())],
        compiler_params=_cparams(("arbitrary",)),
    )(dest.reshape(n_tok // ts, 1, 2 * ts), h)


def _combine_kernel(dest_ref, gate_ref, x_ref, mod_ref, lng_ref, lnb_ref, yb_ref, xo_ref, r0_ref, r1_ref, sem,
                    *, ts, alpha):
    bufs = (r0_ref, r1_ref)

    def start(rb, carry):
        for u in range(ROW_UNROLL):
            r = rb * ROW_UNROLL + u
            for s in range(2):
                _row_copy(yb_ref.at[pl.ds(dest_ref[0, 0, 2 * r + s], 1), :], bufs[s].at[pl.ds(r, 1), :],
                          sem).start(priority=s)
        return carry

    lax.fori_loop(0, ts // ROW_UNROLL, start, 0)
    for s in range(2):
        _row_copy(yb_ref.at[pl.ds(0, ts), :], bufs[s], sem).wait()

    y = gate_ref[:, 0:1] * r0_ref[...] + gate_ref[:, 1:2] * r1_ref[...]
    xo_ref[...] = _resid_ln(x_ref[...], y, mod_ref[0, 5:6, :], lng_ref[...], lnb_ref[...], alpha)


def _combine(dest, gates, x2d, mod_l, ln_g, ln_b, yb, seq, ts, alpha):
    n_tok, d = x2d.shape
    tpb = seq // ts
    kern = functools.partial(_combine_kernel, ts=ts, alpha=alpha)
    return pl.pallas_call(
        kern,
        grid=(n_tok // ts,),
        in_specs=[
            pl.BlockSpec((1, 1, 2 * ts), lambda i: (i, 0, 0), memory_space=pltpu.SMEM),
            pl.BlockSpec((ts, 2), lambda i: (i, 0)),
            pl.BlockSpec((ts, d), lambda i: (i, 0)),
            pl.BlockSpec((1, 6, d), lambda i: (i // tpb, 0, 0)),
            pl.BlockSpec((1, d), lambda i: (0, 0)),
            pl.BlockSpec((1, d), lambda i: (0, 0)),
            pl.BlockSpec(memory_space=pl.ANY),
        ],
        out_specs=pl.BlockSpec((ts, d), lambda i: (i, 0)),
        out_shape=jax.ShapeDtypeStruct((n_tok, d), F32),
        scratch_shapes=[pltpu.VMEM((ts, d), F32), pltpu.VMEM((ts, d), F32), pltpu.SemaphoreType.DMA(())],
        compiler_params=_cparams(("arbitrary",)),
    )(dest.reshape(n_tok // ts, 1, 2 * ts), gates, x2d, mod_l, ln_g.reshape(1, d), ln_b.reshape(1, d), yb)


def _route(logits, tm_e):
    n_tok = logits.shape[0]
    top_val, top_idx = lax.top_k(logits, 2)
    gates = jax.nn.softmax(top_val, axis=-1)
    e_flat = top_idx.reshape(-1).astype(I32)
    onehot = (e_flat[:, None] == jnp.arange(N_EXPERTS, dtype=I32)[None, :]).astype(I32)
    csum = jnp.cumsum(onehot, axis=0)
    counts = csum[-1]
    rank = jnp.sum(csum * onehot, axis=1) - 1
    padded = ((counts + tm_e - 1) // tm_e) * tm_e
    pend = jnp.cumsum(padded)
    pstart = pend - padded
    dest = (pstart[e_flat] + rank).astype(I32)
    n_tiles = (2 * n_tok) // tm_e + N_EXPERTS
    n_used = (pend[-1] // tm_e).astype(I32)
    tile_start = jnp.arange(n_tiles, dtype=I32) * tm_e
    tile_expert = jnp.minimum(jnp.searchsorted(pend, tile_start, side='right'), N_EXPERTS - 1).astype(I32)
    last_e = tile_expert[jnp.maximum(n_used - 1, 0)]
    tile_expert = jnp.where(jnp.arange(n_tiles) < n_used, tile_expert, last_e)
    return dest, gates, tile_expert, n_used.reshape(1), n_tiles


def _reorder_w_in(w_in_l):
    widths = (512, 64, 64, 512, 64, 8, 256, 256, 512, 512, 16)
    offs = [0]
    for wd_ in widths:
        offs.append(offs[-1] + wd_)
    seg = [w_in_l[:, offs[k]:offs[k + 1]] for k in range(len(widths))]
    aq, ak, av, iq, ik, iw, bq, bk, bv, bg, ba = seg
    d = w_in_l.shape[0]
    z = lambda n: jnp.zeros((d, n), w_in_l.dtype)
    aq = aq * (1.4426950408889634 * A_HEAD_DIM ** -0.5)
    w_r = jnp.concatenate([aq, iq, ak, z(64), ik, z(64), bq, bk, bv, bg, iw, ba, z(104)], axis=1).astype(BF16)
    return w_r, jnp.transpose(av).astype(BF16)


def kernel(x, c, w_in, w_a2, b_a, gla_norm_g, w_out, w_mod, b_mod, ln_g, ln_b, ffn_w_gate, ffn_w_up, ffn_w_down,
           w_router, b_router, moe_w_gate, moe_w_up, moe_w_down):
    bsz, seq, d = x.shape
    depth = w_in.shape[0]
    dff = ffn_w_gate.shape[2]
    m = bsz * seq
    alpha = (2 * depth) ** 0.25

    tm_p = min(512, seq)
    tm_f = min(1024, seq)
    tm_e = min(1024, seq)
    T = min(128, seq)
    tk = min(512, seq)
    R = min(512, seq)
    ts = min(512, seq)
    tf = dff // 2 if (dff // 2) % 128 == 0 else dff

    mod = _modulation(c, w_mod, b_mod).reshape(depth, bsz, 6, d)
    x2d = x.reshape(m, d)
    for l in range(depth):
        mod_l = mod[l]
        w_r, w_vt = _reorder_w_in(w_in[l])
        p1, p2, p3, vt = _proj(x2d, mod_l, w_r, w_vt, seq, tm_p, tk)
        ya = _dsa(p1, p3, vt, bsz, seq, T, tk)
        ob = _gla(p2, p3, w_a2[l], b_a[l], gla_norm_g[l], bsz, seq, R)
        j = l // 2
        if l % 2 == 0:
            x1, h2 = _outproj(ya, ob, x2d, w_out[l].astype(BF16), mod_l, ln_g[l, 0], ln_b[l, 0], None,
                              seq, tm_p, alpha)
            x2d = _ffn_dense(h2, x1, ffn_w_gate[j].astype(BF16), ffn_w_up[j].astype(BF16),
                             ffn_w_down[j].astype(BF16), mod_l, ln_g[l, 1], ln_b[l, 1], seq, tm_f, tf, alpha)
        else:
            wr = jnp.zeros((d, 128), F32).at[:, :N_EXPERTS].set(w_router[j])
            x1, h2, lg = _outproj(ya, ob, x2d, w_out[l].astype(BF16), mod_l, ln_g[l, 0], ln_b[l, 0], wr,
                                  seq, tm_p, alpha)
            logits = lg[:, :N_EXPERTS] + b_router[j][None, :]
            dest, gates, tile_expert, n_used, n_tiles = _route(logits, tm_e)
            xs = _scatter_rows(h2, dest, n_tiles * tm_e, ts)
            yb = _ffn_grouped(xs, tile_expert, n_used, moe_w_gate[j].astype(BF16), moe_w_up[j].astype(BF16),
                              moe_w_down[j].astype(BF16), tm_e, tf)
            x2d = _combine(dest, gates, x1, mod_l, ln_g[l, 1], ln_b[l, 1], yb, seq, ts, alpha)
    return x2d.reshape(bsz, seq, d)
```

```python
import functools

import jax
import jax.numpy as jnp
from jax import lax
from jax.experimental import pallas as pl
from jax.experimental.pallas import tpu as pltpu

F32 = jnp.float32
BF16 = jnp.bfloat16
I32 = jnp.int32
HIGHEST = lax.Precision.HIGHEST

A_HEADS = 8
A_HEAD_DIM = 64
IDX_HEADS = 8
IDX_DIM = 64
TOPK_MAX = 256
B_HEADS = 4
B_KEY_DIM = 64
B_VAL_DIM = 128
GATE_RANK = 16
GATE_TAU = 16.0
GLA_CHUNK = 64
N_EXPERTS = 8
NORM_EPS = 1e-5

N1, N2, N3 = 1280, 1024, 640
IW_OFF = 0
BA_OFF = 8

ONES_ROWS = 16
LANE_GROUP = 256
VMEM_LIMIT = 56 * 1024 * 1024
INT_MIN = -2147483648
NEG_BIG = -1e30


def _sigmoid(x):
    return 1.0 / (1.0 + jnp.exp(-x))


def _cparams(sem):
    return pltpu.CompilerParams(dimension_semantics=sem, vmem_limit_bytes=VMEM_LIMIT)


def _mod_kernel(c_ref, w_ref, b_ref, o_ref):
    c = c_ref[...]
    ca = c * _sigmoid(c)
    o_ref[0] = jnp.dot(ca, w_ref[0], precision=HIGHEST, preferred_element_type=F32) + b_ref[0]


def _modulation(c, w_mod, b_mod):
    depth, d, n = w_mod.shape
    bsz = c.shape[0]
    tn = 1536 if n % 1536 == 0 else n
    return pl.pallas_call(
        _mod_kernel,
        grid=(depth, n // tn),
        in_specs=[
            pl.BlockSpec((bsz, d), lambda l, j: (0, 0)),
            pl.BlockSpec((1, d, tn), lambda l, j: (l, 0, j)),
            pl.BlockSpec((1, 1, tn), lambda l, j: (l, 0, j)),
        ],
        out_specs=pl.BlockSpec((1, bsz, tn), lambda l, j: (l, 0, j)),
        out_shape=jax.ShapeDtypeStruct((depth, bsz, n), F32),
        compiler_params=_cparams(("arbitrary", "arbitrary")),
    )(c, w_mod, b_mod.reshape(depth, 1, n))


def _proj_kernel(x_ref, mod_ref, w_ref, wvt_ref, o1_ref, o2_ref, o3_ref, vt_ref, *, tk):
    x = x_ref[...]
    sh = mod_ref[0, 0:1, :]
    sc = mod_ref[0, 1:2, :]
    h = (x * (1.0 + sc) + sh).astype(BF16)
    o1_ref[...] = jnp.dot(h, w_ref[:, 0:N1], preferred_element_type=F32).astype(BF16)
    o2_ref[...] = jnp.dot(h, w_ref[:, N1:N1 + N2], preferred_element_type=F32).astype(BF16)
    o3_ref[...] = jnp.dot(h, w_ref[:, N1 + N2:N1 + N2 + N3], preferred_element_type=F32)
    nt = (((1,), (1,)), ((), ()))
    for c in range(vt_ref.shape[0]):
        vt_ref[c] = lax.dot_general(wvt_ref[...], h[c * tk:(c + 1) * tk, :], nt,
                                    preferred_element_type=F32).astype(BF16)


def _proj(x2d, mod_l, w_in_r, w_vt, seq, tm, tk):
    m, d = x2d.shape
    tpb = seq // tm
    nw = N1 + N2 + N3
    kern = functools.partial(_proj_kernel, tk=tk)
    return pl.pallas_call(
        kern,
        grid=(m // tm,),
        in_specs=[
            pl.BlockSpec((tm, d), lambda i: (i, 0)),
            pl.BlockSpec((1, 6, d), lambda i: (i // tpb, 0, 0)),
            pl.BlockSpec((d, nw), lambda i: (0, 0)),
            pl.BlockSpec((A_HEAD_DIM, d), lambda i: (0, 0)),
        ],
        out_specs=[
            pl.BlockSpec((tm, N1), lambda i: (i, 0)),
            pl.BlockSpec((tm, N2), lambda i: (i, 0)),
            pl.BlockSpec((tm, N3), lambda i: (i, 0)),
            pl.BlockSpec((tm // tk, A_HEAD_DIM, tk), lambda i: (i, 0, 0)),
        ],
        out_shape=[
            jax.ShapeDtypeStruct((m, N1), BF16),
            jax.ShapeDtypeStruct((m, N2), BF16),
            jax.ShapeDtypeStruct((m, N3), F32),
            jax.ShapeDtypeStruct((m // tk, A_HEAD_DIM, tk), BF16),
        ],
        compiler_params=_cparams(("arbitrary",)),
    )(x2d, mod_l, w_in_r, w_vt)


def _bit_transpose32(words):
    a = list(words)
    mask, j = 0x0000FFFF, 16
    while j:
        k = 0
        while k < 32:
            t = (a[k] ^ lax.shift_right_logical(a[k + j], jnp.int32(j))) & mask
            a[k] = a[k] ^ t
            a[k + j] = a[k + j] ^ lax.shift_left(t, jnp.int32(j))
            k = (k + j + 1) & ~j
        j >>= 1
        mask = (mask ^ (mask << j)) & 0xFFFFFFFF
    return a


def _tree_sum(xs):
    xs = list(xs)
    while len(xs) > 1:
        xs = [xs[i] + xs[i + 1] for i in range(0, len(xs) - 1, 2)] + ([xs[-1]] if len(xs) % 2 else [])
    return xs[0]


def _dsa_kernel(q_ref, kv_ref, vt_ref, w_ref, qext_ref, postab_ref, o_ref,
                kx_ref, vtx_ref, qx_ref, qis_ref, keys_ref, planes_ref, m_ref, acc_ref, *, T, tk, k_top):
    i = pl.program_id(1)
    t0 = i * T
    log_tk = tk.bit_length() - 1
    nk = lax.shift_right_logical(t0 + T + tk - 1, log_tk)
    G = LANE_GROUP // T
    NG = A_HEADS // G
    WG = tk // 256
    nt = (((1,), (1,)), ((), ()))

    @pl.when(i == 0)
    def _():
        kx_ref[...] = kv_ref[:, 0:128] + postab_ref[...]
        qx_ref[:, 64:128] = qext_ref[...]
        vtx_ref[:, 0:A_HEAD_DIM, :] = vt_ref[...]
        vtx_ref[:, A_HEAD_DIM:, :] = jnp.ones((vtx_ref.shape[0], ONES_ROWS, tk), BF16)
        planes_ref[...] = jnp.zeros(planes_ref.shape, I32)

    for h in range(A_HEADS):
        qx_ref[h * T:(h + 1) * T, 0:64] = q_ref[:, h * A_HEAD_DIM:(h + 1) * A_HEAD_DIM]
        qis_ref[h * T:(h + 1) * T, :] = q_ref[:, 512 + h * IDX_DIM:512 + (h + 1) * IDX_DIM]

    row = lax.broadcasted_iota(I32, (tk, T), 0)
    col = lax.broadcasted_iota(I32, (tk, T), 1)
    cmr = col - row

    w_t = jnp.transpose(w_ref[...]) * ((IDX_HEADS ** -0.5) * (IDX_DIM ** -0.5))
    w_rows = [w_t[IW_OFF + h:IW_OFF + h + 1, :] for h in range(IDX_HEADS)]

    def p1_body(j, carry):
        off = pl.multiple_of(j * tk, tk)
        ki = kv_ref[pl.ds(off, tk), 128:192]
        acc = jnp.zeros((tk, T), F32)
        for g in range(NG):
            s = lax.dot_general(ki, qis_ref[g * LANE_GROUP:(g + 1) * LANE_GROUP, :], nt,
                                preferred_element_type=F32)
            for hh in range(G):
                acc = acc + w_rows[g * G + hh] * jnp.maximum(s[:, hh * T:(hh + 1) * T], 0.0)
        bits = pltpu.bitcast(acc, I32)
        key = jnp.where(bits >= 0, bits, bits ^ 0x7FFFFFFF)
        valid = (cmr + (t0 - j * tk)) >= 0
        key = jnp.where(valid, key, INT_MIN)
        keys_ref[j] = key.reshape(tk // 8, 8, T)
        u = key ^ INT_MIN
        for wg in range(WG):
            words = [u[(wg * 32 + r) * 8:(wg * 32 + r + 1) * 8, :] for r in range(32)]
            planes = _bit_transpose32(words)
            for q in range(32):
                planes_ref[j, wg, q] = planes[q]
        return carry

    lax.fori_loop(0, nk, p1_body, 0)

    kf = float(k_top)
    nkt = keys_ref.shape[0]
    alive = [jnp.where(j < nk, jnp.full((8, T), -1, I32), jnp.zeros((8, T), I32))
             for j in range(nkt) for _ in range(WG)]
    cnt_above = jnp.zeros((1, T), F32)
    thr_u = jnp.zeros((1, T), I32)
    for q in range(32):
        ones = [alive[j * WG + wg] & planes_ref[j, wg, q] for j in range(nkt) for wg in range(WG)]
        c1 = jnp.sum(_tree_sum([lax.population_count(o) for o in ones]).astype(F32), axis=0, keepdims=True)
        ok = (cnt_above + c1) >= kf
        ok_b = jnp.broadcast_to(ok, (8, T))
        alive = [jnp.where(ok_b, o, a ^ o) for a, o in zip(alive, ones)]
        cnt_above = jnp.where(ok, cnt_above, cnt_above + c1)
        thr_u = thr_u | jnp.where(ok, jnp.int32(INT_MIN if q == 0 else 1 << (31 - q)), 0)
    n_tie = jnp.sum(_tree_sum([lax.population_count(a) for a in alive]).astype(F32), axis=0, keepdims=True)
    need = kf - cnt_above
    thr = jnp.maximum(thr_u ^ INT_MIN, INT_MIN + 1)

    surplus = jnp.where(thr_u != 0, n_tie - need, 0.0)
    has_ties = jnp.max(surplus) > 0.0

    @pl.when(has_ties)
    def _():
        thr_t = jnp.broadcast_to(thr, (tk, T))
        r2 = lax.broadcasted_iota(I32, (tk, tk), 0)
        c2 = lax.broadcasted_iota(I32, (tk, tk), 1)
        tri = jnp.where(r2 >= c2, 1.0, 0.0).astype(BF16)

        def fix_body(j, base):
            tile = keys_ref[j].reshape(tk, T)
            tie = tile == thr_t
            pref = jnp.dot(tri, jnp.where(tie, 1.0, 0.0).astype(BF16), preferred_element_type=F32) + base
            drop = jnp.logical_and(tie, pref > need)
            keys_ref[j] = jnp.where(drop, tile - 1, tile).reshape(tk // 8, 8, T)
            return pref[tk - 1:tk, :]

        lax.fori_loop(0, nk, fix_body, jnp.zeros((1, T), F32))

    m_ref[...] = jnp.full(m_ref.shape, NEG_BIG, F32)
    acc_ref[...] = jnp.zeros(acc_ref.shape, F32)
    thr_b = jnp.broadcast_to(thr, (tk, T))

    def p3_body(j, carry):
        off = pl.multiple_of(j * tk, tk)
        kx = kx_ref[pl.ds(off, tk), :]
        vtx = vtx_ref[j]
        sel = keys_ref[j].reshape(tk, T) >= thr_b
        sel_g = jnp.concatenate([sel] * G, axis=1) if G > 1 else sel
        s_all = [lax.dot_general(kx, qx_ref[g * LANE_GROUP:(g + 1) * LANE_GROUP, :], nt,
                                 preferred_element_type=F32) for g in range(NG)]
        for g in range(NG):
            s = jnp.where(sel_g, s_all[g], NEG_BIG)
            m_old = m_ref[g]
            m_new = jnp.maximum(m_old, jnp.max(s, axis=0, keepdims=True))
            p = jnp.exp2(s - m_new).astype(BF16)
            acc_ref[g] = jnp.exp2(m_old - m_new) * acc_ref[g] + jnp.dot(vtx, p, preferred_element_type=F32)
            m_ref[g] = m_new
        return carry

    lax.fori_loop(0, nk, p3_body, 0)

    def head_out(h):
        g, hh = h // G, h % G
        a = acc_ref[g]
        return a[0:A_HEAD_DIM, hh * T:(hh + 1) * T] / a[A_HEAD_DIM:A_HEAD_DIM + 1, hh * T:(hh + 1) * T]

    for h in range(0, A_HEADS, 2):
        pair = jnp.concatenate([head_out(h), head_out(h + 1)], axis=0)
        o_ref[:, h * A_HEAD_DIM:(h + 2) * A_HEAD_DIM] = jnp.transpose(pair).astype(BF16)


def _alibi_tables(seq, T):
    log2e = 1.4426950408889634
    pos = jnp.arange(seq, dtype=I32)
    hi = (pos // 64).astype(F32)
    lo = (pos % 64).astype(F32)
    z = jnp.zeros((seq,), F32)
    pos_cols = [z] * 64 + [hi, hi, hi, lo, lo, lo] + [z] * 58
    postab = jnp.stack(pos_cols, axis=1).astype(BF16)
    rows = []
    for h in range(A_HEADS):
        c = jnp.asarray(2.0 ** (-8.0 * (h + 1) / A_HEADS) * log2e, F32)
        c1 = c.astype(BF16)
        c2 = (c - c1.astype(F32)).astype(BF16)
        c3 = (c - c1.astype(F32) - c2.astype(F32)).astype(BF16)
        r = jnp.zeros((64,), BF16)
        r = r.at[0].set(c1 * 64).at[1].set(c2 * 64).at[2].set(c3 * 64).at[3].set(c1).at[4].set(c2).at[5].set(c3)
        rows.append(jnp.broadcast_to(r[None, :], (T, 64)))
    qext = jnp.concatenate(rows, axis=0)
    return postab, qext


def _dsa(p1, p3, vt, bsz, seq, T, tk):
    m = p1.shape[0]
    nq = seq // T
    nkt = seq // tk
    k_top = min(TOPK_MAX, seq // 4)
    postab, qext = _alibi_tables(seq, T)
    kern = functools.partial(_dsa_kernel, T=T, tk=tk, k_top=k_top)
    ng = A_HEADS * T // LANE_GROUP
    return pl.pallas_call(
        kern,
        grid=(bsz, nq),
        in_specs=[
            pl.BlockSpec((T, 1024), lambda b, i: (b * nq + i, 0)),
            pl.BlockSpec((seq, 256), lambda b, i: (b, 4)),
            pl.BlockSpec((nkt, A_HEAD_DIM, tk), lambda b, i: (b, 0, 0)),
            pl.BlockSpec((T, 128), lambda b, i: (b * nq + i, 4)),
            pl.BlockSpec((A_HEADS * T, 64), lambda b, i: (0, 0)),
            pl.BlockSpec((seq, 128), lambda b, i: (0, 0)),
        ],
        out_specs=pl.BlockSpec((T, 512), lambda b, i: (b * nq + i, 0)),
        out_shape=jax.ShapeDtypeStruct((m, 512), BF16),
        scratch_shapes=[
            pltpu.VMEM((seq, 128), BF16),
            pltpu.VMEM((nkt, A_HEAD_DIM + ONES_ROWS, tk), BF16),
            pltpu.VMEM((A_HEADS * T, 128), BF16),
            pltpu.VMEM((A_HEADS * T, IDX_DIM), BF16),
            pltpu.VMEM((nkt, tk // 8, 8, T), I32),
            pltpu.VMEM((nkt, tk // 256, 32, 8, T), I32),
            pltpu.VMEM((ng, 1, LANE_GROUP), F32),
            pltpu.VMEM((ng, A_HEAD_DIM + ONES_ROWS, LANE_GROUP), F32),
        ],
        compiler_params=_cparams(("arbitrary", "arbitrary")),
    )(p1, p1, vt, p3, qext, postab)


def _gla_kernel(p2_ref, p3_ref, wa2_ref, ba_ref, g_ref, o_ref, st_ref, *, R):
    C = GLA_CHUNK

    @pl.when(pl.program_id(1) == 0)
    def _():
        st_ref[...] = jnp.zeros(st_ref.shape, F32)

    ba = p3_ref[:, 512 + BA_OFF:512 + BA_OFF + GATE_RANK]
    xg = jnp.dot(ba, wa2_ref[...], precision=HIGHEST, preferred_element_type=F32) + ba_ref[...]
    log_a = (jnp.minimum(xg, 0.0) - jnp.log(1.0 + jnp.exp(-jnp.abs(xg)))) * (1.0 / GATE_TAU)

    r_i = lax.broadcasted_iota(I32, (C, C), 0)
    c_i = lax.broadcasted_iota(I32, (C, C), 1)
    tri_b = r_i >= c_i
    tri_f = tri_b.astype(F32)
    nt = (((1,), (1,)), ((), ()))
    tn = (((0,), (0,)), ((), ()))
    gamma = g_ref[...]

    for c in range(R // C):
        rows = slice(c * C, (c + 1) * C)
        g = log_a[rows, :]
        b = jnp.dot(tri_f, g, precision=HIGHEST, preferred_element_type=F32)
        b_last = b[C - 1:C, :]
        b_mid = b[C // 2 - 1:C // 2, :]
        q = p2_ref[rows, 0:256].astype(F32) * (B_KEY_DIM ** -0.5)
        k = p2_ref[rows, 256:512].astype(F32)
        qe = (q * jnp.exp(b)).astype(BF16)
        qm = (q * jnp.exp(b - b_mid)).astype(BF16)
        km = (k * jnp.exp(b_mid - b)).astype(BF16)
        kl = (k * jnp.exp(b_last - b)).astype(BF16)
        dec = jnp.exp(b_last)
        for h in range(B_HEADS):
            ks = slice(h * B_KEY_DIM, (h + 1) * B_KEY_DIM)
            vs = slice(512 + h * B_VAL_DIM, 512 + (h + 1) * B_VAL_DIM)
            v = p2_ref[rows, vs]
            st = st_ref[h]
            a = lax.dot_general(qm[:, ks], km[:, ks], nt, preferred_element_type=F32)
            a = jnp.where(tri_b, a, 0.0).astype(BF16)
            o = lax.dot_general(qe[:, ks], st.astype(BF16), nt, preferred_element_type=F32)
            o = o + jnp.dot(a, v, preferred_element_type=F32)
            st_ref[h] = st * dec[:, ks] + lax.dot_general(v, kl[:, ks], tn, preferred_element_type=F32)
            ms = jnp.mean(o * o, axis=1, keepdims=True)
            on = o * lax.rsqrt(ms + NORM_EPS) * gamma
            gate = p3_ref[rows, h * B_VAL_DIM:(h + 1) * B_VAL_DIM]
            o_ref[rows, h * B_VAL_DIM:(h + 1) * B_VAL_DIM] = (on * (gate * _sigmoid(gate))).astype(BF16)


def _gla(p2, p3, w_a2, b_a, gamma, bsz, seq, R):
    m = p2.shape[0]
    nr = seq // R
    kern = functools.partial(_gla_kernel, R=R)
    return pl.pallas_call(
        kern,
        grid=(bsz, nr),
        in_specs=[
            pl.BlockSpec((R, N2), lambda b, i: (b * nr + i, 0)),
            pl.BlockSpec((R, N3), lambda b, i: (b * nr + i, 0)),
            pl.BlockSpec((GATE_RANK, 256), lambda b, i: (0, 0)),
            pl.BlockSpec((1, 256), lambda b, i: (0, 0)),
            pl.BlockSpec((1, B_VAL_DIM), lambda b, i: (0, 0)),
        ],
        out_specs=pl.BlockSpec((R, 512), lambda b, i: (b * nr + i, 0)),
        out_shape=jax.ShapeDtypeStruct((m, 512), BF16),
        scratch_shapes=[pltpu.VMEM((B_HEADS, B_VAL_DIM, B_KEY_DIM), F32)],
        compiler_params=_cparams(("arbitrary", "arbitrary")),
    )(p2, p3, w_a2, b_a.reshape(1, 256), gamma.reshape(1, B_VAL_DIM))


def _resid_ln(x, y, gate, ln_g, ln_b, alpha):
    z = alpha * x + (1.0 + gate) * y
    mu = jnp.mean(z, axis=1, keepdims=True)
    zc = z - mu
    var = jnp.mean(zc * zc, axis=1, keepdims=True)
    return zc * lax.rsqrt(var + NORM_EPS) * ln_g + ln_b


def _outproj_kernel(ya_ref, ob_ref, x_ref, w_ref, mod_ref, lng_ref, lnb_ref, *rest, alpha, routed):
    if routed:
        wr_ref, xo_ref, h_ref, lg_ref = rest
    else:
        xo_ref, h_ref = rest
    y = jnp.dot(ya_ref[...], w_ref[0:512, :], preferred_element_type=F32)
    y = y + jnp.dot(ob_ref[...], w_ref[512:1024, :], preferred_element_type=F32)
    xn = _resid_ln(x_ref[...], y, mod_ref[0, 2:3, :], lng_ref[...], lnb_ref[...], alpha)
    xo_ref[...] = xn
    h = xn * (1.0 + mod_ref[0, 4:5, :]) + mod_ref[0, 3:4, :]
    if routed:
        h_ref[...] = h
        lg_ref[...] = jnp.dot(h, wr_ref[...], precision=HIGHEST, preferred_element_type=F32)
    else:
        h_ref[...] = h.astype(BF16)


def _outproj(ya, ob, x2d, w_out, mod_l, ln_g, ln_b, w_router_p, seq, tm, alpha):
    m, d = x2d.shape
    tpb = seq // tm
    routed = w_router_p is not None
    kern = functools.partial(_outproj_kernel, alpha=alpha, routed=routed)
    in_specs = [
        pl.BlockSpec((tm, 512), lambda i: (i, 0)),
        pl.BlockSpec((tm, 512), lambda i: (i, 0)),
        pl.BlockSpec((tm, d), lambda i: (i, 0)),
        pl.BlockSpec((d, d), lambda i: (0, 0)),
        pl.BlockSpec((1, 6, d), lambda i: (i // tpb, 0, 0)),
        pl.BlockSpec((1, d), lambda i: (0, 0)),
        pl.BlockSpec((1, d), lambda i: (0, 0)),
    ]
    args = [ya, ob, x2d, w_out, mod_l, ln_g.reshape(1, d), ln_b.reshape(1, d)]
    out_specs = [pl.BlockSpec((tm, d), lambda i: (i, 0)), pl.BlockSpec((tm, d), lambda i: (i, 0))]
    out_shape = [jax.ShapeDtypeStruct((m, d), F32), jax.ShapeDtypeStruct((m, d), F32 if routed else BF16)]
    if routed:
        in_specs.append(pl.BlockSpec((d, 128), lambda i: (0, 0)))
        args.append(w_router_p)
        out_specs.append(pl.BlockSpec((tm, 128), lambda i: (i, 0)))
        out_shape.append(jax.ShapeDtypeStruct((m, 128), F32))
    return pl.pallas_call(
        kern,
        grid=(m // tm,),
        in_specs=in_specs,
        out_specs=out_specs,
        out_shape=out_shape,
        compiler_params=_cparams(("arbitrary",)),
    )(*args)


def _swiglu_acc(h, wg_ref, wu_ref, wd_ref, acc_ref, tf, widx):
    c0 = 0
    while c0 < tf:
        c1 = min(c0 + 512, tf)
        if widx is None:
            wg, wu, wd = wg_ref[:, c0:c1], wu_ref[:, c0:c1], wd_ref[c0:c1, :]
        else:
            wg, wu, wd = wg_ref[0, :, c0:c1], wu_ref[0, :, c0:c1], wd_ref[0, c0:c1, :]
        g = jnp.dot(h, wg, preferred_element_type=F32)
        u = jnp.dot(h, wu, preferred_element_type=F32)
        a = (g * _sigmoid(g) * u).astype(BF16)
        acc_ref[...] += jnp.dot(a, wd, preferred_element_type=F32)
        c0 = c1


def _ffn_dense_kernel(h_ref, x_ref, wg_ref, wu_ref, wd_ref, mod_ref, lng_ref, lnb_ref, xo_ref, acc_ref,
                      *, tf, alpha):
    f = pl.program_id(1)

    @pl.when(f == 0)
    def _():
        acc_ref[...] = jnp.zeros(acc_ref.shape, F32)

    _swiglu_acc(h_ref[...], wg_ref, wu_ref, wd_ref, acc_ref, tf, None)

    @pl.when(f == pl.num_programs(1) - 1)
    def _():
        xo_ref[...] = _resid_ln(x_ref[...], acc_ref[...], mod_ref[0, 5:6, :], lng_ref[...], lnb_ref[...], alpha)


def _ffn_dense(h, x2d, wg, wu, wd, mod_l, ln_g, ln_b, seq, tm, tf, alpha):
    m, d = x2d.shape
    dff = wg.shape[1]
    tpb = seq // tm
    kern = functools.partial(_ffn_dense_kernel, tf=tf, alpha=alpha)
    return pl.pallas_call(
        kern,
        grid=(m // tm, dff // tf),
        in_specs=[
            pl.BlockSpec((tm, d), lambda i, f: (i, 0)),
            pl.BlockSpec((tm, d), lambda i, f: (i, 0)),
            pl.BlockSpec((d, tf), lambda i, f: (0, f)),
            pl.BlockSpec((d, tf), lambda i, f: (0, f)),
            pl.BlockSpec((tf, d), lambda i, f: (f, 0)),
            pl.BlockSpec((1, 6, d), lambda i, f: (i // tpb, 0, 0)),
            pl.BlockSpec((1, d), lambda i, f: (0, 0)),
            pl.BlockSpec((1, d), lambda i, f: (0, 0)),
        ],
        out_specs=pl.BlockSpec((tm, d), lambda i, f: (i, 0)),
        out_shape=jax.ShapeDtypeStruct((m, d), F32),
        scratch_shapes=[pltpu.VMEM((tm, d), F32)],
        compiler_params=_cparams(("arbitrary", "arbitrary")),
    )(h, x2d, wg, wu, wd, mod_l, ln_g.reshape(1, d), ln_b.reshape(1, d))


def _ffn_grouped_kernel(te_ref, nu_ref, xs_ref, wg_ref, wu_ref, wd_ref, yo_ref, acc_ref, *, tf):
    i = pl.program_id(0)
    f = pl.program_id(1)
    used = i < nu_ref[0]

    @pl.when(jnp.logical_and(used, f == 0))
    def _():
        acc_ref[...] = jnp.zeros(acc_ref.shape, F32)

    @pl.when(used)
    def _():
        _swiglu_acc(xs_ref[...].astype(BF16), wg_ref, wu_ref, wd_ref, acc_ref, tf, 0)

    @pl.when(jnp.logical_and(used, f == pl.num_programs(1) - 1))
    def _():
        yo_ref[...] = acc_ref[...]


def _ffn_grouped(xs, tile_expert, n_used, wg, wu, wd, tm, tf):
    n_rows, d = xs.shape
    dff = wg.shape[2]
    nf = dff // tf

    def w_map(i, f, te, nu):
        return (te[i], 0, jnp.where(i < nu[0], f, nf - 1))

    def wd_map(i, f, te, nu):
        return (te[i], jnp.where(i < nu[0], f, nf - 1), 0)

    kern = functools.partial(_ffn_grouped_kernel, tf=tf)
    return pl.pallas_call(
        kern,
        grid_spec=pltpu.PrefetchScalarGridSpec(
            num_scalar_prefetch=2,
            grid=(n_rows // tm, nf),
            in_specs=[
                pl.BlockSpec((tm, d), lambda i, f, te, nu: (i, 0)),
                pl.BlockSpec((1, d, tf), w_map),
                pl.BlockSpec((1, d, tf), w_map),
                pl.BlockSpec((1, tf, d), wd_map),
            ],
            out_specs=pl.BlockSpec((tm, d), lambda i, f, te, nu: (i, 0)),
            scratch_shapes=[pltpu.VMEM((tm, d), F32)],
        ),
        out_shape=jax.ShapeDtypeStruct((n_rows, d), F32),
        compiler_params=_cparams(("arbitrary", "arbitrary")),
    )(tile_expert, n_used, xs, wg, wu, wd)


def _row_copy(src, dst, sem):
    return pltpu.make_async_copy(src, dst, sem)


ROW_UNROLL = 8


def _scatter_kernel(dest_ref, h_ref, xs_ref, sem, *, ts):
    def start(rb, carry):
        for u in range(ROW_UNROLL):
            r = rb * ROW_UNROLL + u
            for s in range(2):
                _row_copy(h_ref.at[pl.ds(r, 1), :], xs_ref.at[pl.ds(dest_ref[0, 0, 2 * r + s], 1), :],
                          sem).start(priority=s)
        return carry

    lax.fori_loop(0, ts // ROW_UNROLL, start, 0)
    for s in range(2):
        _row_copy(h_ref, xs_ref.at[pl.ds(0, ts), :], sem).wait()


def _scatter_rows(h, dest, n_rows, ts):
    n_tok, d = h.shape
    kern = functools.partial(_scatter_kernel, ts=ts)
    return pl.pallas_call(
        kern,
        grid=(n_tok // ts,),
        in_specs=[
            pl.BlockSpec((1, 1, 2 * ts), lambda i: (i, 0, 0), memory_space=pltpu.SMEM),
            pl.BlockSpec((ts, d), lambda i: (i, 0)),
        ],
        out_specs=pl.BlockSpec(memory_space=pl.ANY),
        out_shape=jax.ShapeDtypeStruct((n_rows, d), F32),
        scratch_shapes=[pltpu.SemaphoreType.DMA(())],
        compiler_params=_cparams(("arbitrary",)),
    )(dest.reshape(n_tok // ts, 1, 2 * ts), h)


def _combine_kernel(dest_ref, gate_ref, x_ref, mod_ref, lng_ref, lnb_ref, yb_ref, xo_ref, r0_ref, r1_ref, sem,
                    *, ts, alpha):
    bufs = (r0_ref, r1_ref)

    def start(rb, carry):
        for u in range(ROW_UNROLL):
            r = rb * ROW_UNROLL + u
            for s in range(2):
                _row_copy(yb_ref.at[pl.ds(dest_ref[0, 0, 2 * r + s], 1), :], bufs[s].at[pl.ds(r, 1), :],
                          sem).start(priority=s)
        return carry

    lax.fori_loop(0, ts // ROW_UNROLL, start, 0)
    for s in range(2):
        _row_copy(yb_ref.at[pl.ds(0, ts), :], bufs[s], sem).wait()

    y = gate_ref[:, 0:1] * r0_ref[...] + gate_ref[:, 1:2] * r1_ref[...]
    xo_ref[...] = _resid_ln(x_ref[...], y, mod_ref[0, 5:6, :], lng_ref[...], lnb_ref[...], alpha)


def _combine(dest, gates, x2d, mod_l, ln_g, ln_b, yb, seq, ts, alpha):
    n_tok, d = x2d.shape
    tpb = seq // ts
    kern = functools.partial(_combine_kernel, ts=ts, alpha=alpha)
    return pl.pallas_call(
        kern,
        grid=(n_tok // ts,),
        in_specs=[
            pl.BlockSpec((1, 1, 2 * ts), lambda i: (i, 0, 0), memory_space=pltpu.SMEM),
            pl.BlockSpec((ts, 2), lambda i: (i, 0)),
            pl.BlockSpec((ts, d), lambda i: (i, 0)),
            pl.BlockSpec((1, 6, d), lambda i: (i // tpb, 0, 0)),
            pl.BlockSpec((1, d), lambda i: (0, 0)),
            pl.BlockSpec((1, d), lambda i: (0, 0)),
            pl.BlockSpec(memory_space=pl.ANY),
        ],
        out_specs=pl.BlockSpec((ts, d), lambda i: (i, 0)),
        out_shape=jax.ShapeDtypeStruct((n_tok, d), F32),
        scratch_shapes=[pltpu.VMEM((ts, d), F32), pltpu.VMEM((ts, d), F32), pltpu.SemaphoreType.DMA(())],
        compiler_params=_cparams(("arbitrary",)),
    )(dest.reshape(n_tok // ts, 1, 2 * ts), gates, x2d, mod_l, ln_g.reshape(1, d), ln_b.reshape(1, d), yb)


def _route(logits, tm_e):
    n_tok = logits.shape[0]
    top_val, top_idx = lax.top_k(logits, 2)
    gates = jax.nn.softmax(top_val, axis=-1)
    e_flat = top_idx.reshape(-1).astype(I32)
    onehot = (e_flat[:, None] == jnp.arange(N_EXPERTS, dtype=I32)[None, :]).astype(I32)
    csum = jnp.cumsum(onehot, axis=0)
    counts = csum[-1]
    rank = jnp.sum(csum * onehot, axis=1) - 1
    padded = ((counts + tm_e - 1) // tm_e) * tm_e
    pend = jnp.cumsum(padded)
    pstart = pend - padded
    dest = (pstart[e_flat] + rank).astype(I32)
    n_tiles = (2 * n_tok) // tm_e + N_EXPERTS
    n_used = (pend[-1] // tm_e).astype(I32)
    tile_start = jnp.arange(n_tiles, dtype=I32) * tm_e
    tile_expert = jnp.minimum(jnp.searchsorted(pend, tile_start, side='right'), N_EXPERTS - 1).astype(I32)
    last_e = tile_expert[jnp.maximum(n_used - 1, 0)]
    tile_expert = jnp.where(jnp.arange(n_tiles) < n_used, tile_expert, last_e)
    return dest, gates, tile_expert, n_used.reshape(1), n_tiles


def _reorder_w_in(w_in_l):
    widths = (512, 64, 64, 512, 64, 8, 256, 256, 512, 512, 16)
    offs = [0]
    for wd_ in widths:
        offs.append(offs[-1] + wd_)
    seg = [w_in_l[:, offs[k]:offs[k + 1]] for k in range(len(widths))]
    aq, ak, av, iq, ik, iw, bq, bk, bv, bg, ba = seg
    d = w_in_l.shape[0]
    z = lambda n: jnp.zeros((d, n), w_in_l.dtype)
    aq = aq * (1.4426950408889634 * A_HEAD_DIM ** -0.5)
    w_r = jnp.concatenate([aq, iq, ak, z(64), ik, z(64), bq, bk, bv, bg, iw, ba, z(104)], axis=1).astype(BF16)
    return w_r, jnp.transpose(av).astype(BF16)


def kernel(x, c, w_in, w_a2, b_a, gla_norm_g, w_out, w_mod, b_mod, ln_g, ln_b, ffn_w_gate, ffn_w_up, ffn_w_down,
           w_router, b_router, moe_w_gate, moe_w_up, moe_w_down):
    bsz, seq, d = x.shape
    depth = w_in.shape[0]
    dff = ffn_w_gate.shape[2]
    m = bsz * seq
    alpha = (2 * depth) ** 0.25

    tm_p = min(512, seq)
    tm_f = min(1024, seq)
    tm_e = min(1024, seq)
    T = min(128, seq)
    tk = min(512, seq)
    R = min(512, seq)
    ts = min(512, seq)
    tf = dff // 2 if (dff // 2) % 128 == 0 else dff

    mod = _modulation(c, w_mod, b_mod).reshape(depth, bsz, 6, d)
    x2d = x.reshape(m, d)
    for l in range(depth):
        mod_l = mod[l]
        w_r, w_vt = _reorder_w_in(w_in[l])
        p1, p2, p3, vt = _proj(x2d, mod_l, w_r, w_vt, seq, tm_p, tk)
        ya = _dsa(p1, p3, vt, bsz, seq, T, tk)
        ob = _gla(p2, p3, w_a2[l], b_a[l], gla_norm_g[l], bsz, seq, R)
        j = l // 2
        if l % 2 == 0:
            x1, h2 = _outproj(ya, ob, x2d, w_out[l].astype(BF16), mod_l, ln_g[l, 0], ln_b[l, 0], None,
                              seq, tm_p, alpha)
            x2d = _ffn_dense(h2, x1, ffn_w_gate[j].astype(BF16), ffn_w_up[j].astype(BF16),
                             ffn_w_down[j].astype(BF16), mod_l, ln_g[l, 1], ln_b[l, 1], seq, tm_f, tf, alpha)
        else:
            wr = jnp.zeros((d, 128), F32).at[:, :N_EXPERTS].set(w_router[j])
            x1, h2, lg = _outproj(ya, ob, x2d, w_out[l].astype(BF16), mod_l, ln_g[l, 0], ln_b[l, 0], wr,
                                  seq, tm_p, alpha)
            logits = lg[:, :N_EXPERTS] + b_router[j][None, :]
            dest, gates, tile_expert, n_used, n_tiles = _route(logits, tm_e)
            xs = _scatter_rows(h2, dest, n_tiles * tm_e, ts)
            yb = _ffn_grouped(xs, tile_expert, n_used, moe_w_gate[j].astype(BF16), moe_w_up[j].astype(BF16),
                              moe_w_down[j].astype(BF16), tm_e, tf)
            x2d = _combine(dest, gates, x1, mod_l, ln_g[l, 1], ln_b[l, 1], yb, seq, ts, alpha)
    return x2d.reshape(bsz, seq, d)
```

```python
import functools

import jax
import jax.numpy as jnp
from jax import lax
from jax.experimental import pallas as pl
from jax.experimental.pallas import tpu as pltpu

F32 = jnp.float32
BF16 = jnp.bfloat16
I32 = jnp.int32
HIGHEST = lax.Precision.HIGHEST

A_HEADS = 8
A_HEAD_DIM = 64
IDX_HEADS = 8
IDX_DIM = 64
TOPK_MAX = 256
B_HEADS = 4
B_KEY_DIM = 64
B_VAL_DIM = 128
GATE_RANK = 16
GATE_TAU = 16.0
GLA_CHUNK = 64
N_EXPERTS = 8
NORM_EPS = 1e-5

N1, N2, N3 = 1280, 1024, 640
IW_OFF = 0
BA_OFF = 8

ONES_ROWS = 16
LANE_GROUP = 256
VMEM_LIMIT = 56 * 1024 * 1024
INT_MIN = -2147483648
NEG_BIG = -1e30


def _sigmoid(x):
    return 1.0 / (1.0 + jnp.exp(-x))


def _cparams(sem):
    return pltpu.CompilerParams(dimension_semantics=sem, vmem_limit_bytes=VMEM_LIMIT)


def _mod_kernel(c_ref, w_ref, b_ref, o_ref):
    c = c_ref[...]
    ca = c * _sigmoid(c)
    o_ref[0] = jnp.dot(ca, w_ref[0], precision=HIGHEST, preferred_element_type=F32) + b_ref[0]


def _modulation(c, w_mod, b_mod):
    depth, d, n = w_mod.shape
    bsz = c.shape[0]
    tn = 1536 if n % 1536 == 0 else n
    return pl.pallas_call(
        _mod_kernel,
        grid=(depth, n // tn),
        in_specs=[
            pl.BlockSpec((bsz, d), lambda l, j: (0, 0)),
            pl.BlockSpec((1, d, tn), lambda l, j: (l, 0, j)),
            pl.BlockSpec((1, 1, tn), lambda l, j: (l, 0, j)),
        ],
        out_specs=pl.BlockSpec((1, bsz, tn), lambda l, j: (l, 0, j)),
        out_shape=jax.ShapeDtypeStruct((depth, bsz, n), F32),
        compiler_params=_cparams(("arbitrary", "arbitrary")),
    )(c, w_mod, b_mod.reshape(depth, 1, n))


def _proj_kernel(x_ref, mod_ref, w_ref, wvt_ref, o1_ref, o2_ref, o3_ref, vt_ref, *, tk):
    x = x_ref[...]
    sh = mod_ref[0, 0:1, :]
    sc = mod_ref[0, 1:2, :]
    h = (x * (1.0 + sc) + sh).astype(BF16)
    o1_ref[...] = jnp.dot(h, w_ref[:, 0:N1], preferred_element_type=F32).astype(BF16)
    o2_ref[...] = jnp.dot(h, w_ref[:, N1:N1 + N2], preferred_element_type=F32).astype(BF16)
    o3_ref[...] = jnp.dot(h, w_ref[:, N1 + N2:N1 + N2 + N3], preferred_element_type=F32)
    nt = (((1,), (1,)), ((), ()))
    for c in range(vt_ref.shape[0]):
        vt_ref[c] = lax.dot_general(wvt_ref[...], h[c * tk:(c + 1) * tk, :], nt,
                                    preferred_element_type=F32).astype(BF16)


def _proj(x2d, mod_l, w_in_r, w_vt, seq, tm, tk):
    m, d = x2d.shape
    tpb = seq // tm
    nw = N1 + N2 + N3
    kern = functools.partial(_proj_kernel, tk=tk)
    return pl.pallas_call(
        kern,
        grid=(m // tm,),
        in_specs=[
            pl.BlockSpec((tm, d), lambda i: (i, 0)),
            pl.BlockSpec((1, 6, d), lambda i: (i // tpb, 0, 0)),
            pl.BlockSpec((d, nw), lambda i: (0, 0)),
            pl.BlockSpec((A_HEAD_DIM, d), lambda i: (0, 0)),
        ],
        out_specs=[
            pl.BlockSpec((tm, N1), lambda i: (i, 0)),
            pl.BlockSpec((tm, N2), lambda i: (i, 0)),
            pl.BlockSpec((tm, N3), lambda i: (i, 0)),
            pl.BlockSpec((tm // tk, A_HEAD_DIM, tk), lambda i: (i, 0, 0)),
        ],
        out_shape=[
            jax.ShapeDtypeStruct((m, N1), BF16),
            jax.ShapeDtypeStruct((m, N2), BF16),
            jax.ShapeDtypeStruct((m, N3), F32),
            jax.ShapeDtypeStruct((m // tk, A_HEAD_DIM, tk), BF16),
        ],
        compiler_params=_cparams(("arbitrary",)),
    )(x2d, mod_l, w_in_r, w_vt)


def _bit_transpose32(words):
    a = list(words)
    mask, j = 0x0000FFFF, 16
    while j:
        k = 0
        while k < 32:
            t = (a[k] ^ lax.shift_right_logical(a[k + j], jnp.int32(j))) & mask
            a[k] = a[k] ^ t
            a[k + j] = a[k + j] ^ lax.shift_left(t, jnp.int32(j))
            k = (k + j + 1) & ~j
        j >>= 1
        mask = (mask ^ (mask << j)) & 0xFFFFFFFF
    return a


def _tree_sum(xs):
    xs = list(xs)
    while len(xs) > 1:
        xs = [xs[i] + xs[i + 1] for i in range(0, len(xs) - 1, 2)] + ([xs[-1]] if len(xs) % 2 else [])
    return xs[0]


def _dsa_kernel(q_ref, kv_ref, vt_ref, w_ref, qext_ref, postab_ref, o_ref,
                kx_ref, vtx_ref, qx_ref, qis_ref, keys_ref, planes_ref, m_ref, acc_ref, *, T, tk, k_top):
    i = pl.program_id(1)
    t0 = i * T
    log_tk = tk.bit_length() - 1
    nk = lax.shift_right_logical(t0 + T + tk - 1, log_tk)
    G = LANE_GROUP // T
    NG = A_HEADS // G
    WG = tk // 256
    nt = (((1,), (1,)), ((), ()))

    @pl.when(i == 0)
    def _():
        kx_ref[...] = kv_ref[:, 0:128] + postab_ref[...]
        qx_ref[:, 64:128] = qext_ref[...]
        vtx_ref[:, 0:A_HEAD_DIM, :] = vt_ref[...]
        vtx_ref[:, A_HEAD_DIM:, :] = jnp.ones((vtx_ref.shape[0], ONES_ROWS, tk), BF16)
        planes_ref[...] = jnp.zeros(planes_ref.shape, I32)

    for h in range(A_HEADS):
        qx_ref[h * T:(h + 1) * T, 0:64] = q_ref[:, h * A_HEAD_DIM:(h + 1) * A_HEAD_DIM]
        qis_ref[h * T:(h + 1) * T, :] = q_ref[:, 512 + h * IDX_DIM:512 + (h + 1) * IDX_DIM]

    row = lax.broadcasted_iota(I32, (tk, T), 0)
    col = lax.broadcasted_iota(I32, (tk, T), 1)
    cmr = col - row

    w_t = jnp.transpose(w_ref[...]) * ((IDX_HEADS ** -0.5) * (IDX_DIM ** -0.5))
    w_rows = [w_t[IW_OFF + h:IW_OFF + h + 1, :] for h in range(IDX_HEADS)]

    def p1_body(j, carry):
        off = pl.multiple_of(j * tk, tk)
        ki = kv_ref[pl.ds(off, tk), 128:192]
        acc = jnp.zeros((tk, T), F32)
        for g in range(NG):
            s = lax.dot_general(ki, qis_ref[g * LANE_GROUP:(g + 1) * LANE_GROUP, :], nt,
                                preferred_element_type=F32)
            for hh in range(G):
                acc = acc + w_rows[g * G + hh] * jnp.maximum(s[:, hh * T:(hh + 1) * T], 0.0)
        bits = pltpu.bitcast(acc, I32)
        key = jnp.where(bits >= 0, bits, bits ^ 0x7FFFFFFF)
        valid = (cmr + (t0 - j * tk)) >= 0
        key = jnp.where(valid, key, INT_MIN)
        keys_ref[j] = key.reshape(tk // 8, 8, T)
        u = key ^ INT_MIN
        for wg in range(WG):
            words = [u[(wg * 32 + r) * 8:(wg * 32 + r + 1) * 8, :] for r in range(32)]
            planes = _bit_transpose32(words)
            for q in range(32):
                planes_ref[j, wg, q] = planes[q]
        return carry

    lax.fori_loop(0, nk, p1_body, 0)

    kf = float(k_top)
    nkt = keys_ref.shape[0]
    alive = [jnp.where(j < nk, jnp.full((8, T), -1, I32), jnp.zeros((8, T), I32))
             for j in range(nkt) for _ in range(WG)]
    cnt_above = jnp.zeros((1, T), F32)
    thr_u = jnp.zeros((1, T), I32)
    for q in range(32):
        ones = [alive[j * WG + wg] & planes_ref[j, wg, q] for j in range(nkt) for wg in range(WG)]
        c1 = jnp.sum(_tree_sum([lax.population_count(o) for o in ones]).astype(F32), axis=0, keepdims=True)
        ok = (cnt_above + c1) >= kf
        ok_b = jnp.broadcast_to(ok, (8, T))
        alive = [jnp.where(ok_b, o, a ^ o) for a, o in zip(alive, ones)]
        cnt_above = jnp.where(ok, cnt_above, cnt_above + c1)
        thr_u = thr_u | jnp.where(ok, jnp.int32(INT_MIN if q == 0 else 1 << (31 - q)), 0)
    n_tie = jnp.sum(_tree_sum([lax.population_count(a) for a in alive]).astype(F32), axis=0, keepdims=True)
    need = kf - cnt_above
    thr = jnp.maximum(thr_u ^ INT_MIN, INT_MIN + 1)

    surplus = jnp.where(thr_u != 0, n_tie - need, 0.0)
    has_ties = jnp.max(surplus) > 0.0

    @pl.when(has_ties)
    def _():
        thr_t = jnp.broadcast_to(thr, (tk, T))
        r2 = lax.broadcasted_iota(I32, (tk, tk), 0)
        c2 = lax.broadcasted_iota(I32, (tk, tk), 1)
        tri = jnp.where(r2 >= c2, 1.0, 0.0).astype(BF16)

        def fix_body(j, base):
            tile = keys_ref[j].reshape(tk, T)
            tie = tile == thr_t
            pref = jnp.dot(tri, jnp.where(tie, 1.0, 0.0).astype(BF16), preferred_element_type=F32) + base
            drop = jnp.logical_and(tie, pref > need)
            keys_ref[j] = jnp.where(drop, tile - 1, tile).reshape(tk // 8, 8, T)
            return pref[tk - 1:tk, :]

        lax.fori_loop(0, nk, fix_body, jnp.zeros((1, T), F32))

    m_ref[...] = jnp.full(m_ref.shape, NEG_BIG, F32)
    acc_ref[...] = jnp.zeros(acc_ref.shape, F32)
    thr_b = jnp.broadcast_to(thr, (tk, T))

    def tile_logits(j):
        off = pl.multiple_of(j * tk, tk)
        kx = kx_ref[pl.ds(off, tk), :]
        return [lax.dot_general(kx, qx_ref[g * LANE_GROUP:(g + 1) * LANE_GROUP, :], nt,
                                preferred_element_type=F32) for g in range(NG)]

    def tile_softmax(j, s_all):
        vtx = vtx_ref[j]
        sel = keys_ref[j].reshape(tk, T) >= thr_b
        sel_g = jnp.concatenate([sel] * G, axis=1) if G > 1 else sel
        for g in range(NG):
            s = jnp.where(sel_g, s_all[g], NEG_BIG)
            m_old = m_ref[g]
            m_new = jnp.maximum(m_old, jnp.max(s, axis=0, keepdims=True))
            p = jnp.exp2(s - m_new).astype(BF16)
            acc_ref[g] = jnp.exp2(m_old - m_new) * acc_ref[g] + jnp.dot(vtx, p, preferred_element_type=F32)
            m_ref[g] = m_new

    def p3_pair(jj, carry):
        s_a = tile_logits(2 * jj)
        s_b = tile_logits(2 * jj + 1)
        tile_softmax(2 * jj, s_a)
        tile_softmax(2 * jj + 1, s_b)
        return carry

    lax.fori_loop(0, lax.shift_right_logical(nk, 1), p3_pair, 0)

    @pl.when((nk & 1) == 1)
    def _():
        tile_softmax(nk - 1, tile_logits(nk - 1))

    def head_out(h):
        g, hh = h // G, h % G
        a = acc_ref[g]
        return a[0:A_HEAD_DIM, hh * T:(hh + 1) * T] / a[A_HEAD_DIM:A_HEAD_DIM + 1, hh * T:(hh + 1) * T]

    for h in range(0, A_HEADS, 2):
        pair = jnp.concatenate([head_out(h), head_out(h + 1)], axis=0)
        o_ref[:, h * A_HEAD_DIM:(h + 2) * A_HEAD_DIM] = jnp.transpose(pair).astype(BF16)


def _alibi_tables(seq, T):
    log2e = 1.4426950408889634
    pos = jnp.arange(seq, dtype=I32)
    hi = (pos // 64).astype(F32)
    lo = (pos % 64).astype(F32)
    z = jnp.zeros((seq,), F32)
    pos_cols = [z] * 64 + [hi, hi, hi, lo, lo, lo] + [z] * 58
    postab = jnp.stack(pos_cols, axis=1).astype(BF16)
    rows = []
    for h in range(A_HEADS):
        c = jnp.asarray(2.0 ** (-8.0 * (h + 1) / A_HEADS) * log2e, F32)
        c1 = c.astype(BF16)
        c2 = (c - c1.astype(F32)).astype(BF16)
        c3 = (c - c1.astype(F32) - c2.astype(F32)).astype(BF16)
        r = jnp.zeros((64,), BF16)
        r = r.at[0].set(c1 * 64).at[1].set(c2 * 64).at[2].set(c3 * 64).at[3].set(c1).at[4].set(c2).at[5].set(c3)
        rows.append(jnp.broadcast_to(r[None, :], (T, 64)))
    qext = jnp.concatenate(rows, axis=0)
    return postab, qext


def _dsa(p1, p3, vt, bsz, seq, T, tk):
    m = p1.shape[0]
    nq = seq // T
    nkt = seq // tk
    k_top = min(TOPK_MAX, seq // 4)
    postab, qext = _alibi_tables(seq, T)
    kern = functools.partial(_dsa_kernel, T=T, tk=tk, k_top=k_top)
    ng = A_HEADS * T // LANE_GROUP
    return pl.pallas_call(
        kern,
        grid=(bsz, nq),
        in_specs=[
            pl.BlockSpec((T, 1024), lambda b, i: (b * nq + i, 0)),
            pl.BlockSpec((seq, 256), lambda b, i: (b, 4)),
            pl.BlockSpec((nkt, A_HEAD_DIM, tk), lambda b, i: (b, 0, 0)),
            pl.BlockSpec((T, 128), lambda b, i: (b * nq + i, 4)),
            pl.BlockSpec((A_HEADS * T, 64), lambda b, i: (0, 0)),
            pl.BlockSpec((seq, 128), lambda b, i: (0, 0)),
        ],
        out_specs=pl.BlockSpec((T, 512), lambda b, i: (b * nq + i, 0)),
        out_shape=jax.ShapeDtypeStruct((m, 512), BF16),
        scratch_shapes=[
            pltpu.VMEM((seq, 128), BF16),
            pltpu.VMEM((nkt, A_HEAD_DIM + ONES_ROWS, tk), BF16),
            pltpu.VMEM((A_HEADS * T, 128), BF16),
            pltpu.VMEM((A_HEADS * T, IDX_DIM), BF16),
            pltpu.VMEM((nkt, tk // 8, 8, T), I32),
            pltpu.VMEM((nkt, tk // 256, 32, 8, T), I32),
            pltpu.VMEM((ng, 1, LANE_GROUP), F32),
            pltpu.VMEM((ng, A_HEAD_DIM + ONES_ROWS, LANE_GROUP), F32),
        ],
        compiler_params=_cparams(("arbitrary", "arbitrary")),
    )(p1, p1, vt, p3, qext, postab)


def _gla_kernel(p2_ref, p3_ref, wa2_ref, ba_ref, g_ref, o_ref, st_ref, *, R):
    C = GLA_CHUNK

    @pl.when(pl.program_id(1) == 0)
    def _():
        st_ref[...] = jnp.zeros(st_ref.shape, F32)

    ba = p3_ref[:, 512 + BA_OFF:512 + BA_OFF + GATE_RANK]
    xg = jnp.dot(ba, wa2_ref[...], precision=HIGHEST, preferred_element_type=F32) + ba_ref[...]
    log_a = (jnp.minimum(xg, 0.0) - jnp.log(1.0 + jnp.exp(-jnp.abs(xg)))) * (1.0 / GATE_TAU)

    r_i = lax.broadcasted_iota(I32, (C, C), 0)
    c_i = lax.broadcasted_iota(I32, (C, C), 1)
    tri_b = r_i >= c_i
    tri_f = tri_b.astype(F32)
    nt = (((1,), (1,)), ((), ()))
    tn = (((0,), (0,)), ((), ()))
    gamma = g_ref[...]

    sts = [st_ref[h] for h in range(B_HEADS)]
    heads = range(B_HEADS)
    kss = [slice(h * B_KEY_DIM, (h + 1) * B_KEY_DIM) for h in heads]
    bs = [jnp.dot(tri_f, log_a[c * C:(c + 1) * C, :], precision=HIGHEST, preferred_element_type=F32)
          for c in range(R // C)]
    for c in range(R // C):
        rows = slice(c * C, (c + 1) * C)
        b = bs[c]
        b_last = b[C - 1:C, :]
        b_mid = b[C // 2 - 1:C // 2, :]
        q = p2_ref[rows, 0:256].astype(F32) * (B_KEY_DIM ** -0.5)
        k = p2_ref[rows, 256:512].astype(F32)
        qe = (q * jnp.exp(b)).astype(BF16)
        qm = (q * jnp.exp(b - b_mid)).astype(BF16)
        km = (k * jnp.exp(b_mid - b)).astype(BF16)
        kl = (k * jnp.exp(b_last - b)).astype(BF16)
        dec = jnp.exp(b_last)
        vs_ = [p2_ref[rows, 512 + h * B_VAL_DIM:512 + (h + 1) * B_VAL_DIM] for h in heads]
        a_s = [lax.dot_general(qm[:, kss[h]], km[:, kss[h]], nt, preferred_element_type=F32) for h in heads]
        o_s = [lax.dot_general(qe[:, kss[h]], sts[h].astype(BF16), nt, preferred_element_type=F32) for h in heads]
        x_s = [lax.dot_general(vs_[h], kl[:, kss[h]], tn, preferred_element_type=F32) for h in heads]
        a_s = [jnp.where(tri_b, a, 0.0).astype(BF16) for a in a_s]
        o_s = [o_s[h] + jnp.dot(a_s[h], vs_[h], preferred_element_type=F32) for h in heads]
        sts = [sts[h] * dec[:, kss[h]] + x_s[h] for h in heads]
        for h in heads:
            o = o_s[h]
            ms = jnp.mean(o * o, axis=1, keepdims=True)
            on = o * lax.rsqrt(ms + NORM_EPS) * gamma
            gate = p3_ref[rows, h * B_VAL_DIM:(h + 1) * B_VAL_DIM]
            o_ref[rows, h * B_VAL_DIM:(h + 1) * B_VAL_DIM] = (on * (gate * _sigmoid(gate))).astype(BF16)
    for h in range(B_HEADS):
        st_ref[h] = sts[h]


def _gla(p2, p3, w_a2, b_a, gamma, bsz, seq, R):
    m = p2.shape[0]
    nr = seq // R
    kern = functools.partial(_gla_kernel, R=R)
    return pl.pallas_call(
        kern,
        grid=(bsz, nr),
        in_specs=[
            pl.BlockSpec((R, N2), lambda b, i: (b * nr + i, 0)),
            pl.BlockSpec((R, N3), lambda b, i: (b * nr + i, 0)),
            pl.BlockSpec((GATE_RANK, 256), lambda b, i: (0, 0)),
            pl.BlockSpec((1, 256), lambda b, i: (0, 0)),
            pl.BlockSpec((1, B_VAL_DIM), lambda b, i: (0, 0)),
        ],
        out_specs=pl.BlockSpec((R, 512), lambda b, i: (b * nr + i, 0)),
        out_shape=jax.ShapeDtypeStruct((m, 512), BF16),
        scratch_shapes=[pltpu.VMEM((B_HEADS, B_VAL_DIM, B_KEY_DIM), F32)],
        compiler_params=_cparams(("arbitrary", "arbitrary")),
    )(p2, p3, w_a2, b_a.reshape(1, 256), gamma.reshape(1, B_VAL_DIM))


def _resid_ln(x, y, gate, ln_g, ln_b, alpha):
    z = alpha * x + (1.0 + gate) * y
    mu = jnp.mean(z, axis=1, keepdims=True)
    zc = z - mu
    var = jnp.mean(zc * zc, axis=1, keepdims=True)
    return zc * lax.rsqrt(var + NORM_EPS) * ln_g + ln_b


def _outproj_kernel(ya_ref, ob_ref, x_ref, w_ref, mod_ref, lng_ref, lnb_ref, *rest, alpha, routed):
    if routed:
        wr_ref, xo_ref, h_ref, lg_ref = rest
    else:
        xo_ref, h_ref = rest
    y = jnp.dot(ya_ref[...], w_ref[0:512, :], preferred_element_type=F32)
    y = y + jnp.dot(ob_ref[...], w_ref[512:1024, :], preferred_element_type=F32)
    xn = _resid_ln(x_ref[...], y, mod_ref[0, 2:3, :], lng_ref[...], lnb_ref[...], alpha)
    xo_ref[...] = xn
    h = xn * (1.0 + mod_ref[0, 4:5, :]) + mod_ref[0, 3:4, :]
    if routed:
        h_ref[...] = h
        lg_ref[...] = jnp.dot(h, wr_ref[...], precision=HIGHEST, preferred_element_type=F32)
    else:
        h_ref[...] = h.astype(BF16)


def _outproj(ya, ob, x2d, w_out, mod_l, ln_g, ln_b, w_router_p, seq, tm, alpha):
    m, d = x2d.shape
    tpb = seq // tm
    routed = w_router_p is not None
    kern = functools.partial(_outproj_kernel, alpha=alpha, routed=routed)
    in_specs = [
        pl.BlockSpec((tm, 512), lambda i: (i, 0)),
        pl.BlockSpec((tm, 512), lambda i: (i, 0)),
        pl.BlockSpec((tm, d), lambda i: (i, 0)),
        pl.BlockSpec((d, d), lambda i: (0, 0)),
        pl.BlockSpec((1, 6, d), lambda i: (i // tpb, 0, 0)),
        pl.BlockSpec((1, d), lambda i: (0, 0)),
        pl.BlockSpec((1, d), lambda i: (0, 0)),
    ]
    args = [ya, ob, x2d, w_out, mod_l, ln_g.reshape(1, d), ln_b.reshape(1, d)]
    out_specs = [pl.BlockSpec((tm, d), lambda i: (i, 0)), pl.BlockSpec((tm, d), lambda i: (i, 0))]
    out_shape = [jax.ShapeDtypeStruct((m, d), F32), jax.ShapeDtypeStruct((m, d), F32 if routed else BF16)]
    if routed:
        in_specs.append(pl.BlockSpec((d, 128), lambda i: (0, 0)))
        args.append(w_router_p)
        out_specs.append(pl.BlockSpec((tm, 128), lambda i: (i, 0)))
        out_shape.append(jax.ShapeDtypeStruct((m, 128), F32))
    return pl.pallas_call(
        kern,
        grid=(m // tm,),
        in_specs=in_specs,
        out_specs=out_specs,
        out_shape=out_shape,
        compiler_params=_cparams(("arbitrary",)),
    )(*args)


def _swiglu_acc(h, wg_ref, wu_ref, wd_ref, acc_ref, tf, widx):
    c0 = 0
    while c0 < tf:
        c1 = min(c0 + 512, tf)
        if widx is None:
            wg, wu, wd = wg_ref[:, c0:c1], wu_ref[:, c0:c1], wd_ref[c0:c1, :]
        else:
            wg, wu, wd = wg_ref[0, :, c0:c1], wu_ref[0, :, c0:c1], wd_ref[0, c0:c1, :]
        g = jnp.dot(h, wg, preferred_element_type=F32)
        u = jnp.dot(h, wu, preferred_element_type=F32)
        a = (g * _sigmoid(g) * u).astype(BF16)
        acc_ref[...] += jnp.dot(a, wd, preferred_element_type=F32)
        c0 = c1


def _ffn_dense_kernel(h_ref, x_ref, wg_ref, wu_ref, wd_ref, mod_ref, lng_ref, lnb_ref, xo_ref, acc_ref,
                      *, tf, alpha):
    f = pl.program_id(1)

    @pl.when(f == 0)
    def _():
        acc_ref[...] = jnp.zeros(acc_ref.shape, F32)

    _swiglu_acc(h_ref[...], wg_ref, wu_ref, wd_ref, acc_ref, tf, None)

    @pl.when(f == pl.num_programs(1) - 1)
    def _():
        xo_ref[...] = _resid_ln(x_ref[...], acc_ref[...], mod_ref[0, 5:6, :], lng_ref[...], lnb_ref[...], alpha)


def _ffn_dense(h, x2d, wg, wu, wd, mod_l, ln_g, ln_b, seq, tm, tf, alpha):
    m, d = x2d.shape
    dff = wg.shape[1]
    tpb = seq // tm
    kern = functools.partial(_ffn_dense_kernel, tf=tf, alpha=alpha)
    return pl.pallas_call(
        kern,
        grid=(m // tm, dff // tf),
        in_specs=[
            pl.BlockSpec((tm, d), lambda i, f: (i, 0)),
            pl.BlockSpec((tm, d), lambda i, f: (i, 0)),
            pl.BlockSpec((d, tf), lambda i, f: (0, f)),
            pl.BlockSpec((d, tf), lambda i, f: (0, f)),
            pl.BlockSpec((tf, d), lambda i, f: (f, 0)),
            pl.BlockSpec((1, 6, d), lambda i, f: (i // tpb, 0, 0)),
            pl.BlockSpec((1, d), lambda i, f: (0, 0)),
            pl.BlockSpec((1, d), lambda i, f: (0, 0)),
        ],
        out_specs=pl.BlockSpec((tm, d), lambda i, f: (i, 0)),
        out_shape=jax.ShapeDtypeStruct((m, d), F32),
        scratch_shapes=[pltpu.VMEM((tm, d), F32)],
        compiler_params=_cparams(("arbitrary", "arbitrary")),
    )(h, x2d, wg, wu, wd, mod_l, ln_g.reshape(1, d), ln_b.reshape(1, d))


def _ffn_grouped_kernel(te_ref, nu_ref, xs_ref, wg_ref, wu_ref, wd_ref, yo_ref, acc_ref, *, tf):
    i = pl.program_id(0)
    f = pl.program_id(1)
    used = i < nu_ref[0]

    @pl.when(jnp.logical_and(used, f == 0))
    def _():
        acc_ref[...] = jnp.zeros(acc_ref.shape, F32)

    @pl.when(used)
    def _():
        _swiglu_acc(xs_ref[...].astype(BF16), wg_ref, wu_ref, wd_ref, acc_ref, tf, 0)

    @pl.when(jnp.logical_and(used, f == pl.num_programs(1) - 1))
    def _():
        yo_ref[...] = acc_ref[...]


def _ffn_grouped(xs, tile_expert, n_used, wg, wu, wd, tm, tf):
    n_rows, d = xs.shape
    dff = wg.shape[2]
    nf = dff // tf

    def w_map(i, f, te, nu):
        return (te[i], 0, jnp.where(i < nu[0], f, nf - 1))

    def wd_map(i, f, te, nu):
        return (te[i], jnp.where(i < nu[0], f, nf - 1), 0)

    kern = functools.partial(_ffn_grouped_kernel, tf=tf)
    return pl.pallas_call(
        kern,
        grid_spec=pltpu.PrefetchScalarGridSpec(
            num_scalar_prefetch=2,
            grid=(n_rows // tm, nf),
            in_specs=[
                pl.BlockSpec((tm, d), lambda i, f, te, nu: (i, 0)),
                pl.BlockSpec((1, d, tf), w_map),
                pl.BlockSpec((1, d, tf), w_map),
                pl.BlockSpec((1, tf, d), wd_map),
            ],
            out_specs=pl.BlockSpec((tm, d), lambda i, f, te, nu: (i, 0)),
            scratch_shapes=[pltpu.VMEM((tm, d), F32)],
        ),
        out_shape=jax.ShapeDtypeStruct((n_rows, d), F32),
        compiler_params=_cparams(("arbitrary", "arbitrary")),
    )(tile_expert, n_used, xs, wg, wu, wd)


def _row_copy(src, dst, sem):
    return pltpu.make_async_copy(src, dst, sem)


ROW_UNROLL = 8


def _scatter_kernel(dest_ref, h_ref, xs_ref, sem, *, ts):
    def start(rb, carry):
        for u in range(ROW_UNROLL):
            r = rb * ROW_UNROLL + u
            for s in range(2):
                _row_copy(h_ref.at[pl.ds(r, 1), :], xs_ref.at[pl.ds(dest_ref[0, 0, 2 * r + s], 1), :],
                          sem).start(priority=s)
        return carry

    lax.fori_loop(0, ts // ROW_UNROLL, start, 0)
    for s in range(2):
        _row_copy(h_ref, xs_ref.at[pl.ds(0, ts), :], sem).wait()


def _scatter_rows(h, dest, n_rows, ts):
    n_tok, d = h.shape
    kern = functools.partial(_scatter_kernel, ts=ts)
    return pl.pallas_call(
        kern,
        grid=(n_tok // ts,),
        in_specs=[
            pl.BlockSpec((1, 1, 2 * ts), lambda i: (i, 0, 0), memory_space=pltpu.SMEM),
            pl.BlockSpec((ts, d), lambda i: (i, 0)),
        ],
        out_specs=pl.BlockSpec(memory_space=pl.ANY),
        out_shape=jax.ShapeDtypeStruct((n_rows, d), F32),
        scratch_shapes=[pltpu.SemaphoreType.DMA(())],
        compiler_params=_cparams(("arbitrary",)),
    )(dest.reshape(n_tok // ts, 1, 2 * ts), h)


def _combine_kernel(dest_ref, gate_ref, x_ref, mod_ref, lng_ref, lnb_ref, yb_ref, xo_ref, r0_ref, r1_ref, sem,
                    *, ts, alpha):
    bufs = (r0_ref, r1_ref)

    def start(rb, carry):
        for u in range(ROW_UNROLL):
            r = rb * ROW_UNROLL + u
            for s in range(2):
                _row_copy(yb_ref.at[pl.ds(dest_ref[0, 0, 2 * r + s], 1), :], bufs[s].at[pl.ds(r, 1), :],
                          sem).start(priority=s)
        return carry

    lax.fori_loop(0, ts // ROW_UNROLL, start, 0)
    for s in range(2):
        _row_copy(yb_ref.at[pl.ds(0, ts), :], bufs[s], sem).wait()

    y = gate_ref[:, 0:1] * r0_ref[...] + gate_ref[:, 1:2] * r1_ref[...]
    xo_ref[...] = _resid_ln(x_ref[...], y, mod_ref[0, 5:6, :], lng_ref[...], lnb_ref[...], alpha)


def _combine(dest, gates, x2d, mod_l, ln_g, ln_b, yb, seq, ts, alpha):
    n_tok, d = x2d.shape
    tpb = seq // ts
    kern = functools.partial(_combine_kernel, ts=ts, alpha=alpha)
    return pl.pallas_call(
        kern,
        grid=(n_tok // ts,),
        in_specs=[
            pl.BlockSpec((1, 1, 2 * ts), lambda i: (i, 0, 0), memory_space=pltpu.SMEM),
            pl.BlockSpec((ts, 2), lambda i: (i, 0)),
            pl.BlockSpec((ts, d), lambda i: (i, 0)),
            pl.BlockSpec((1, 6, d), lambda i: (i // tpb, 0, 0)),
            pl.BlockSpec((1, d), lambda i: (0, 0)),
            pl.BlockSpec((1, d), lambda i: (0, 0)),
            pl.BlockSpec(memory_space=pl.ANY),
        ],
        out_specs=pl.BlockSpec((ts, d), lambda i: (i, 0)),
        out_shape=jax.ShapeDtypeStruct((n_tok, d), F32),
        scratch_shapes=[pltpu.VMEM((ts, d), F32), pltpu.VMEM((ts, d), F32), pltpu.SemaphoreType.DMA(())],
        compiler_params=_cparams(("arbitrary",)),
    )(dest.reshape(n_tok // ts, 1, 2 * ts), gates, x2d, mod_l, ln_g.reshape(1, d), ln_b.reshape(1, d), yb)


def _route(logits, tm_e):
    n_tok = logits.shape[0]
    top_val, top_idx = lax.top_k(logits, 2)
    gates = jax.nn.softmax(top_val, axis=-1)
    e_flat = top_idx.reshape(-1).astype(I32)
    onehot = (e_flat[:, None] == jnp.arange(N_EXPERTS, dtype=I32)[None, :]).astype(I32)
    csum = jnp.cumsum(onehot, axis=0)
    counts = csum[-1]
    rank = jnp.sum(csum * onehot, axis=1) - 1
    padded = ((counts + tm_e - 1) // tm_e) * tm_e
    pend = jnp.cumsum(padded)
    pstart = pend - padded
    dest = (pstart[e_flat] + rank).astype(I32)
    n_tiles = (2 * n_tok) // tm_e + N_EXPERTS
    n_used = (pend[-1] // tm_e).astype(I32)
    tile_start = jnp.arange(n_tiles, dtype=I32) * tm_e
    tile_expert = jnp.minimum(jnp.searchsorted(pend, tile_start, side='right'), N_EXPERTS - 1).astype(I32)
    last_e = tile_expert[jnp.maximum(n_used - 1, 0)]
    tile_expert = jnp.where(jnp.arange(n_tiles) < n_used, tile_expert, last_e)
    return dest, gates, tile_expert, n_used.reshape(1), n_tiles


def _reorder_w_in(w_in_l):
    widths = (512, 64, 64, 512, 64, 8, 256, 256, 512, 512, 16)
    offs = [0]
    for wd_ in widths:
        offs.append(offs[-1] + wd_)
    seg = [w_in_l[:, offs[k]:offs[k + 1]] for k in range(len(widths))]
    aq, ak, av, iq, ik, iw, bq, bk, bv, bg, ba = seg
    d = w_in_l.shape[0]
    z = lambda n: jnp.zeros((d, n), w_in_l.dtype)
    aq = aq * (1.4426950408889634 * A_HEAD_DIM ** -0.5)
    w_r = jnp.concatenate([aq, iq, ak, z(64), ik, z(64), bq, bk, bv, bg, iw, ba, z(104)], axis=1).astype(BF16)
    return w_r, jnp.transpose(av).astype(BF16)


def kernel(x, c, w_in, w_a2, b_a, gla_norm_g, w_out, w_mod, b_mod, ln_g, ln_b, ffn_w_gate, ffn_w_up, ffn_w_down,
           w_router, b_router, moe_w_gate, moe_w_up, moe_w_down):
    bsz, seq, d = x.shape
    depth = w_in.shape[0]
    dff = ffn_w_gate.shape[2]
    m = bsz * seq
    alpha = (2 * depth) ** 0.25

    tm_p = min(512, seq)
    tm_f = min(1024, seq)
    tm_e = min(1024, seq)
    T = min(128, seq)
    tk = min(512, seq)
    R = min(512, seq)
    ts = min(512, seq)
    tf = dff // 2 if (dff // 2) % 128 == 0 else dff

    mod = _modulation(c, w_mod, b_mod).reshape(depth, bsz, 6, d)
    x2d = x.reshape(m, d)
    for l in range(depth):
        mod_l = mod[l]
        w_r, w_vt = _reorder_w_in(w_in[l])
        p1, p2, p3, vt = _proj(x2d, mod_l, w_r, w_vt, seq, tm_p, tk)
        ya = _dsa(p1, p3, vt, bsz, seq, T, tk)
        ob = _gla(p2, p3, w_a2[l], b_a[l], gla_norm_g[l], bsz, seq, R)
        j = l // 2
        if l % 2 == 0:
            x1, h2 = _outproj(ya, ob, x2d, w_out[l].astype(BF16), mod_l, ln_g[l, 0], ln_b[l, 0], None,
                              seq, tm_p, alpha)
            x2d = _ffn_dense(h2, x1, ffn_w_gate[j].astype(BF16), ffn_w_up[j].astype(BF16),
                             ffn_w_down[j].astype(BF16), mod_l, ln_g[l, 1], ln_b[l, 1], seq, tm_f, tf, alpha)
        else:
            wr = jnp.zeros((d, 128), F32).at[:, :N_EXPERTS].set(w_router[j])
            x1, h2, lg = _outproj(ya, ob, x2d, w_out[l].astype(BF16), mod_l, ln_g[l, 0], ln_b[l, 0], wr,
                                  seq, tm_p, alpha)
            logits = lg[:, :N_EXPERTS] + b_router[j][None, :]
            dest, gates, tile_expert, n_used, n_tiles = _route(logits, tm_e)
            xs = _scatter_rows(h2, dest, n_tiles * tm_e, ts)
            yb = _ffn_grouped(xs, tile_expert, n_used, moe_w_gate[j].astype(BF16), moe_w_up[j].astype(BF16),
                              moe_w_down[j].astype(BF16), tm_e, tf)
            x2d = _combine(dest, gates, x1, mod_l, ln_g[l, 1], ln_b[l, 1], yb, seq, ts, alpha)
    return x2d.reshape(bsz, seq, d)
```

```python
import functools

import jax
import jax.numpy as jnp
from jax import lax
from jax.experimental import pallas as pl
from jax.experimental.pallas import tpu as pltpu

F32 = jnp.float32
BF16 = jnp.bfloat16
I32 = jnp.int32
HIGHEST = lax.Precision.HIGHEST

A_HEADS = 8
A_HEAD_DIM = 64
IDX_HEADS = 8
IDX_DIM = 64
TOPK_MAX = 256
B_HEADS = 4
B_KEY_DIM = 64
B_VAL_DIM = 128
GATE_RANK = 16
GATE_TAU = 16.0
GLA_CHUNK = 64
N_EXPERTS = 8
NORM_EPS = 1e-5

N1, N2, N3 = 1280, 1024, 640
IW_OFF = 0
BA_OFF = 8

ONES_ROWS = 16
LANE_GROUP = 256
VMEM_LIMIT = 56 * 1024 * 1024
INT_MIN = -2147483648
NEG_BIG = -1e30


def _sigmoid(x):
    return 1.0 / (1.0 + jnp.exp(-x))


def _cparams(sem):
    return pltpu.CompilerParams(dimension_semantics=sem, vmem_limit_bytes=VMEM_LIMIT)


def _mod_kernel(c_ref, w_ref, b_ref, o_ref):
    c = c_ref[...]
    ca = c * _sigmoid(c)
    o_ref[0] = jnp.dot(ca, w_ref[0], precision=HIGHEST, preferred_element_type=F32) + b_ref[0]


def _modulation(c, w_mod, b_mod):
    depth, d, n = w_mod.shape
    bsz = c.shape[0]
    tn = 1536 if n % 1536 == 0 else n
    return pl.pallas_call(
        _mod_kernel,
        grid=(depth, n // tn),
        in_specs=[
            pl.BlockSpec((bsz, d), lambda l, j: (0, 0)),
            pl.BlockSpec((1, d, tn), lambda l, j: (l, 0, j)),
            pl.BlockSpec((1, 1, tn), lambda l, j: (l, 0, j)),
        ],
        out_specs=pl.BlockSpec((1, bsz, tn), lambda l, j: (l, 0, j)),
        out_shape=jax.ShapeDtypeStruct((depth, bsz, n), F32),
        compiler_params=_cparams(("arbitrary", "arbitrary")),
    )(c, w_mod, b_mod.reshape(depth, 1, n))


def _proj_kernel(x_ref, mod_ref, w_ref, wvt_ref, o1_ref, o2_ref, o3_ref, vt_ref, *, tk):
    x = x_ref[...]
    sh = mod_ref[0, 0:1, :]
    sc = mod_ref[0, 1:2, :]
    h = (x * (1.0 + sc) + sh).astype(BF16)
    o1_ref[...] = jnp.dot(h, w_ref[:, 0:N1], preferred_element_type=F32).astype(BF16)
    o2_ref[...] = jnp.dot(h, w_ref[:, N1:N1 + N2], preferred_element_type=F32).astype(BF16)
    o3_ref[...] = jnp.dot(h, w_ref[:, N1 + N2:N1 + N2 + N3], preferred_element_type=F32)
    nt = (((1,), (1,)), ((), ()))
    for c in range(vt_ref.shape[0]):
        vt_ref[c] = lax.dot_general(wvt_ref[...], h[c * tk:(c + 1) * tk, :], nt,
                                    preferred_element_type=F32).astype(BF16)


def _proj(x2d, mod_l, w_in_r, w_vt, seq, tm, tk):
    m, d = x2d.shape
    tpb = seq // tm
    nw = N1 + N2 + N3
    kern = functools.partial(_proj_kernel, tk=tk)
    return pl.pallas_call(
        kern,
        grid=(m // tm,),
        in_specs=[
            pl.BlockSpec((tm, d), lambda i: (i, 0)),
            pl.BlockSpec((1, 6, d), lambda i: (i // tpb, 0, 0)),
            pl.BlockSpec((d, nw), lambda i: (0, 0)),
            pl.BlockSpec((A_HEAD_DIM, d), lambda i: (0, 0)),
        ],
        out_specs=[
            pl.BlockSpec((tm, N1), lambda i: (i, 0)),
            pl.BlockSpec((tm, N2), lambda i: (i, 0)),
            pl.BlockSpec((tm, N3), lambda i: (i, 0)),
            pl.BlockSpec((tm // tk, A_HEAD_DIM, tk), lambda i: (i, 0, 0)),
        ],
        out_shape=[
            jax.ShapeDtypeStruct((m, N1), BF16),
            jax.ShapeDtypeStruct((m, N2), BF16),
            jax.ShapeDtypeStruct((m, N3), F32),
            jax.ShapeDtypeStruct((m // tk, A_HEAD_DIM, tk), BF16),
        ],
        compiler_params=_cparams(("arbitrary",)),
    )(x2d, mod_l, w_in_r, w_vt)


def _bit_transpose32(words):
    a = list(words)
    mask, j = 0x0000FFFF, 16
    while j:
        k = 0
        while k < 32:
            t = (a[k] ^ lax.shift_right_logical(a[k + j], jnp.int32(j))) & mask
            a[k] = a[k] ^ t
            a[k + j] = a[k + j] ^ lax.shift_left(t, jnp.int32(j))
            k = (k + j + 1) & ~j
        j >>= 1
        mask = (mask ^ (mask << j)) & 0xFFFFFFFF
    return a


def _tree_sum(xs):
    xs = list(xs)
    while len(xs) > 1:
        xs = [xs[i] + xs[i + 1] for i in range(0, len(xs) - 1, 2)] + ([xs[-1]] if len(xs) % 2 else [])
    return xs[0]


def _dsa_kernel(q_ref, kv_ref, vt_ref, w_ref, qext_ref, postab_ref, o_ref,
                kx_ref, vtx_ref, qx_ref, qis_ref, keys_ref, planes_ref, m_ref, acc_ref, *, T, tk, k_top):
    i = pl.program_id(1)
    t0 = i * T
    log_tk = tk.bit_length() - 1
    nk = lax.shift_right_logical(t0 + T + tk - 1, log_tk)
    G = LANE_GROUP // T
    NG = A_HEADS // G
    WG = tk // 256
    nt = (((1,), (1,)), ((), ()))

    @pl.when(i == 0)
    def _():
        kx_ref[...] = kv_ref[:, 0:128] + postab_ref[...]
        qx_ref[:, 64:128] = qext_ref[...]
        vtx_ref[:, 0:A_HEAD_DIM, :] = vt_ref[...]
        vtx_ref[:, A_HEAD_DIM:, :] = jnp.ones((vtx_ref.shape[0], ONES_ROWS, tk), BF16)
        planes_ref[...] = jnp.zeros(planes_ref.shape, I32)

    for h in range(A_HEADS):
        qx_ref[h * T:(h + 1) * T, 0:64] = q_ref[:, h * A_HEAD_DIM:(h + 1) * A_HEAD_DIM]
        qis_ref[h * T:(h + 1) * T, :] = q_ref[:, 512 + h * IDX_DIM:512 + (h + 1) * IDX_DIM]

    row = lax.broadcasted_iota(I32, (tk, T), 0)
    col = lax.broadcasted_iota(I32, (tk, T), 1)
    cmr = col - row

    w_t = jnp.transpose(w_ref[...]) * ((IDX_HEADS ** -0.5) * (IDX_DIM ** -0.5))
    w_rows = [w_t[IW_OFF + h:IW_OFF + h + 1, :] for h in range(IDX_HEADS)]

    def p1_body(j, carry):
        off = pl.multiple_of(j * tk, tk)
        ki = kv_ref[pl.ds(off, tk), 128:192]
        acc = jnp.zeros((tk, T), F32)
        for g in range(NG):
            s = lax.dot_general(ki, qis_ref[g * LANE_GROUP:(g + 1) * LANE_GROUP, :], nt,
                                preferred_element_type=F32)
            for hh in range(G):
                acc = acc + w_rows[g * G + hh] * jnp.maximum(s[:, hh * T:(hh + 1) * T], 0.0)
        bits = pltpu.bitcast(acc, I32)
        key = jnp.where(bits >= 0, bits, bits ^ 0x7FFFFFFF)
        valid = (cmr + (t0 - j * tk)) >= 0
        key = jnp.where(valid, key, INT_MIN)
        keys_ref[j] = key.reshape(tk // 8, 8, T)
        u = key ^ INT_MIN
        for wg in range(WG):
            words = [u[(wg * 32 + r) * 8:(wg * 32 + r + 1) * 8, :] for r in range(32)]
            planes = _bit_transpose32(words)
            for q in range(32):
                planes_ref[j, wg, q] = planes[q]
        return carry

    lax.fori_loop(0, nk, p1_body, 0)

    kf = float(k_top)
    nkt = keys_ref.shape[0]
    alive = [jnp.where(j < nk, jnp.full((8, T), -1, I32), jnp.zeros((8, T), I32))
             for j in range(nkt) for _ in range(WG)]
    cnt_above = jnp.zeros((1, T), F32)
    thr_u = jnp.zeros((1, T), I32)
    for q in range(32):
        ones = [alive[j * WG + wg] & planes_ref[j, wg, q] for j in range(nkt) for wg in range(WG)]
        c1 = jnp.sum(_tree_sum([lax.population_count(o) for o in ones]).astype(F32), axis=0, keepdims=True)
        ok = (cnt_above + c1) >= kf
        ok_b = jnp.broadcast_to(ok, (8, T))
        alive = [jnp.where(ok_b, o, a ^ o) for a, o in zip(alive, ones)]
        cnt_above = jnp.where(ok, cnt_above, cnt_above + c1)
        thr_u = thr_u | jnp.where(ok, jnp.int32(INT_MIN if q == 0 else 1 << (31 - q)), 0)
    n_tie = jnp.sum(_tree_sum([lax.population_count(a) for a in alive]).astype(F32), axis=0, keepdims=True)
    need = kf - cnt_above
    thr = jnp.maximum(thr_u ^ INT_MIN, INT_MIN + 1)

    surplus = jnp.where(thr_u != 0, n_tie - need, 0.0)
    has_ties = jnp.max(surplus) > 0.0

    @pl.when(has_ties)
    def _():
        thr_t = jnp.broadcast_to(thr, (tk, T))
        r2 = lax.broadcasted_iota(I32, (tk, tk), 0)
        c2 = lax.broadcasted_iota(I32, (tk, tk), 1)
        tri = jnp.where(r2 >= c2, 1.0, 0.0).astype(BF16)

        def fix_body(j, base):
            tile = keys_ref[j].reshape(tk, T)
            tie = tile == thr_t
            pref = jnp.dot(tri, jnp.where(tie, 1.0, 0.0).astype(BF16), preferred_element_type=F32) + base
            drop = jnp.logical_and(tie, pref > need)
            keys_ref[j] = jnp.where(drop, tile - 1, tile).reshape(tk // 8, 8, T)
            return pref[tk - 1:tk, :]

        lax.fori_loop(0, nk, fix_body, jnp.zeros((1, T), F32))

    m_ref[...] = jnp.full(m_ref.shape, NEG_BIG, F32)
    acc_ref[...] = jnp.zeros(acc_ref.shape, F32)
    thr_b = jnp.broadcast_to(thr, (tk, T))

    def tile_logits(j):
        off = pl.multiple_of(j * tk, tk)
        kx = kx_ref[pl.ds(off, tk), :]
        return [lax.dot_general(kx, qx_ref[g * LANE_GROUP:(g + 1) * LANE_GROUP, :], nt,
                                preferred_element_type=F32) for g in range(NG)]

    def tile_softmax(j, s_all):
        vtx = vtx_ref[j]
        sel = keys_ref[j].reshape(tk, T) >= thr_b
        sel_g = jnp.concatenate([sel] * G, axis=1) if G > 1 else sel
        for g in range(NG):
            s = jnp.where(sel_g, s_all[g], NEG_BIG)
            m_old = m_ref[g]
            m_new = jnp.maximum(m_old, jnp.max(s, axis=0, keepdims=True))
            p = jnp.exp2(s - m_new).astype(BF16)
            acc_ref[g] = jnp.exp2(m_old - m_new) * acc_ref[g] + jnp.dot(vtx, p, preferred_element_type=F32)
            m_ref[g] = m_new

    def p3_pair(jj, carry):
        s_a = tile_logits(2 * jj)
        s_b = tile_logits(2 * jj + 1)
        tile_softmax(2 * jj, s_a)
        tile_softmax(2 * jj + 1, s_b)
        return carry

    lax.fori_loop(0, lax.shift_right_logical(nk, 1), p3_pair, 0)

    @pl.when((nk & 1) == 1)
    def _():
        tile_softmax(nk - 1, tile_logits(nk - 1))

    def head_out(h):
        g, hh = h // G, h % G
        a = acc_ref[g]
        return a[0:A_HEAD_DIM, hh * T:(hh + 1) * T] / a[A_HEAD_DIM:A_HEAD_DIM + 1, hh * T:(hh + 1) * T]

    for h in range(0, A_HEADS, 2):
        pair = jnp.concatenate([head_out(h), head_out(h + 1)], axis=0)
        o_ref[:, h * A_HEAD_DIM:(h + 2) * A_HEAD_DIM] = jnp.transpose(pair).astype(BF16)


def _alibi_tables(seq, T):
    log2e = 1.4426950408889634
    pos = jnp.arange(seq, dtype=I32)
    hi = (pos // 64).astype(F32)
    lo = (pos % 64).astype(F32)
    z = jnp.zeros((seq,), F32)
    pos_cols = [z] * 64 + [hi, hi, hi, lo, lo, lo] + [z] * 58
    postab = jnp.stack(pos_cols, axis=1).astype(BF16)
    rows = []
    for h in range(A_HEADS):
        c = jnp.asarray(2.0 ** (-8.0 * (h + 1) / A_HEADS) * log2e, F32)
        c1 = c.astype(BF16)
        c2 = (c - c1.astype(F32)).astype(BF16)
        c3 = (c - c1.astype(F32) - c2.astype(F32)).astype(BF16)
        r = jnp.zeros((64,), BF16)
        r = r.at[0].set(c1 * 64).at[1].set(c2 * 64).at[2].set(c3 * 64).at[3].set(c1).at[4].set(c2).at[5].set(c3)
        rows.append(jnp.broadcast_to(r[None, :], (T, 64)))
    qext = jnp.concatenate(rows, axis=0)
    return postab, qext


def _dsa(p1, p3, vt, bsz, seq, T, tk):
    m = p1.shape[0]
    nq = seq // T
    nkt = seq // tk
    k_top = min(TOPK_MAX, seq // 4)
    postab, qext = _alibi_tables(seq, T)
    kern = functools.partial(_dsa_kernel, T=T, tk=tk, k_top=k_top)
    ng = A_HEADS * T // LANE_GROUP
    return pl.pallas_call(
        kern,
        grid=(bsz, nq),
        in_specs=[
            pl.BlockSpec((T, 1024), lambda b, i: (b * nq + i, 0)),
            pl.BlockSpec((seq, 256), lambda b, i: (b, 4)),
            pl.BlockSpec((nkt, A_HEAD_DIM, tk), lambda b, i: (b, 0, 0)),
            pl.BlockSpec((T, 128), lambda b, i: (b * nq + i, 4)),
            pl.BlockSpec((A_HEADS * T, 64), lambda b, i: (0, 0)),
            pl.BlockSpec((seq, 128), lambda b, i: (0, 0)),
        ],
        out_specs=pl.BlockSpec((T, 512), lambda b, i: (b * nq + i, 0)),
        out_shape=jax.ShapeDtypeStruct((m, 512), BF16),
        scratch_shapes=[
            pltpu.VMEM((seq, 128), BF16),
            pltpu.VMEM((nkt, A_HEAD_DIM + ONES_ROWS, tk), BF16),
            pltpu.VMEM((A_HEADS * T, 128), BF16),
            pltpu.VMEM((A_HEADS * T, IDX_DIM), BF16),
            pltpu.VMEM((nkt, tk // 8, 8, T), I32),
            pltpu.VMEM((nkt, tk // 256, 32, 8, T), I32),
            pltpu.VMEM((ng, 1, LANE_GROUP), F32),
            pltpu.VMEM((ng, A_HEAD_DIM + ONES_ROWS, LANE_GROUP), F32),
        ],
        compiler_params=_cparams(("arbitrary", "arbitrary")),
    )(p1, p1, vt, p3, qext, postab)


def _gla_kernel(p2_ref, p3_ref, wa2_ref, ba_ref, g_ref, o_ref, st_ref, *, R):
    C = GLA_CHUNK

    @pl.when(pl.program_id(1) == 0)
    def _():
        st_ref[...] = jnp.zeros(st_ref.shape, F32)

    ba = p3_ref[:, 512 + BA_OFF:512 + BA_OFF + GATE_RANK]
    xg = jnp.dot(ba, wa2_ref[...], precision=HIGHEST, preferred_element_type=F32) + ba_ref[...]
    log_a = (jnp.minimum(xg, 0.0) - jnp.log(1.0 + jnp.exp(-jnp.abs(xg)))) * (1.0 / GATE_TAU)

    r_i = lax.broadcasted_iota(I32, (C, C), 0)
    c_i = lax.broadcasted_iota(I32, (C, C), 1)
    tri_b = r_i >= c_i
    tri_f = tri_b.astype(F32)
    nt = (((1,), (1,)), ((), ()))
    tn = (((0,), (0,)), ((), ()))
    gamma = g_ref[...]

    sts = [st_ref[h] for h in range(B_HEADS)]
    heads = range(B_HEADS)
    kss = [slice(h * B_KEY_DIM, (h + 1) * B_KEY_DIM) for h in heads]
    bs = [jnp.dot(tri_f, log_a[c * C:(c + 1) * C, :], precision=HIGHEST, preferred_element_type=F32)
          for c in range(R // C)]
    for c in range(R // C):
        rows = slice(c * C, (c + 1) * C)
        b = bs[c]
        b_last = b[C - 1:C, :]
        b_mid = b[C // 2 - 1:C // 2, :]
        q = p2_ref[rows, 0:256].astype(F32) * (B_KEY_DIM ** -0.5)
        k = p2_ref[rows, 256:512].astype(F32)
        qe = (q * jnp.exp(b)).astype(BF16)
        qm = (q * jnp.exp(b - b_mid)).astype(BF16)
        km = (k * jnp.exp(b_mid - b)).astype(BF16)
        kl = (k * jnp.exp(b_last - b)).astype(BF16)
        dec = jnp.exp(b_last)
        vs_ = [p2_ref[rows, 512 + h * B_VAL_DIM:512 + (h + 1) * B_VAL_DIM] for h in heads]
        a_s = [lax.dot_general(qm[:, kss[h]], km[:, kss[h]], nt, preferred_element_type=F32) for h in heads]
        o_s = [lax.dot_general(qe[:, kss[h]], sts[h].astype(BF16), nt, preferred_element_type=F32) for h in heads]
        x_s = [lax.dot_general(vs_[h], kl[:, kss[h]], tn, preferred_element_type=F32) for h in heads]
        a_s = [jnp.where(tri_b, a, 0.0).astype(BF16) for a in a_s]
        o_s = [o_s[h] + jnp.dot(a_s[h], vs_[h], preferred_element_type=F32) for h in heads]
        sts = [sts[h] * dec[:, kss[h]] + x_s[h] for h in heads]
        for h in heads:
            o = o_s[h]
            ms = jnp.mean(o * o, axis=1, keepdims=True)
            on = o * lax.rsqrt(ms + NORM_EPS) * gamma
            gate = p3_ref[rows, h * B_VAL_DIM:(h + 1) * B_VAL_DIM]
            o_ref[rows, h * B_VAL_DIM:(h + 1) * B_VAL_DIM] = (on * (gate * _sigmoid(gate))).astype(BF16)
    for h in range(B_HEADS):
        st_ref[h] = sts[h]


def _gla(p2, p3, w_a2, b_a, gamma, bsz, seq, R):
    m = p2.shape[0]
    nr = seq // R
    kern = functools.partial(_gla_kernel, R=R)
    return pl.pallas_call(
        kern,
        grid=(bsz, nr),
        in_specs=[
            pl.BlockSpec((R, N2), lambda b, i: (b * nr + i, 0)),
            pl.BlockSpec((R, N3), lambda b, i: (b * nr + i, 0)),
            pl.BlockSpec((GATE_RANK, 256), lambda b, i: (0, 0)),
            pl.BlockSpec((1, 256), lambda b, i: (0, 0)),
            pl.BlockSpec((1, B_VAL_DIM), lambda b, i: (0, 0)),
        ],
        out_specs=pl.BlockSpec((R, 512), lambda b, i: (b * nr + i, 0)),
        out_shape=jax.ShapeDtypeStruct((m, 512), BF16),
        scratch_shapes=[pltpu.VMEM((B_HEADS, B_VAL_DIM, B_KEY_DIM), F32)],
        compiler_params=_cparams(("arbitrary", "arbitrary")),
    )(p2, p3, w_a2, b_a.reshape(1, 256), gamma.reshape(1, B_VAL_DIM))


def _resid_ln(x, y, gate, ln_g, ln_b, alpha):
    z = alpha * x + (1.0 + gate) * y
    mu = jnp.mean(z, axis=1, keepdims=True)
    zc = z - mu
    var = jnp.mean(zc * zc, axis=1, keepdims=True)
    return zc * lax.rsqrt(var + NORM_EPS) * ln_g + ln_b


def _outproj_kernel(ya_ref, ob_ref, x_ref, w_ref, mod_ref, lng_ref, lnb_ref, *rest, alpha, routed):
    if routed:
        wr_ref, xo_ref, h_ref, lg_ref = rest
    else:
        xo_ref, h_ref = rest
    y = jnp.dot(ya_ref[...], w_ref[0:512, :], preferred_element_type=F32)
    y = y + jnp.dot(ob_ref[...], w_ref[512:1024, :], preferred_element_type=F32)
    xn = _resid_ln(x_ref[...], y, mod_ref[0, 2:3, :], lng_ref[...], lnb_ref[...], alpha)
    xo_ref[...] = xn
    h = xn * (1.0 + mod_ref[0, 4:5, :]) + mod_ref[0, 3:4, :]
    if routed:
        h_ref[...] = h
        lane = lax.broadcasted_iota(I32, (1, 128), 1)
        lg = jnp.zeros((h.shape[0], 128), F32)
        for e in range(N_EXPERTS):
            col = jnp.sum(h * wr_ref[e:e + 1, :], axis=1, keepdims=True)
            lg = lg + col * jnp.where(lane == e, 1.0, 0.0)
        lg_ref[...] = lg
    else:
        h_ref[...] = h.astype(BF16)


def _outproj(ya, ob, x2d, w_out, mod_l, ln_g, ln_b, w_router_p, seq, tm, alpha):
    m, d = x2d.shape
    tpb = seq // tm
    routed = w_router_p is not None
    kern = functools.partial(_outproj_kernel, alpha=alpha, routed=routed)
    in_specs = [
        pl.BlockSpec((tm, 512), lambda i: (i, 0)),
        pl.BlockSpec((tm, 512), lambda i: (i, 0)),
        pl.BlockSpec((tm, d), lambda i: (i, 0)),
        pl.BlockSpec((d, d), lambda i: (0, 0)),
        pl.BlockSpec((1, 6, d), lambda i: (i // tpb, 0, 0)),
        pl.BlockSpec((1, d), lambda i: (0, 0)),
        pl.BlockSpec((1, d), lambda i: (0, 0)),
    ]
    args = [ya, ob, x2d, w_out, mod_l, ln_g.reshape(1, d), ln_b.reshape(1, d)]
    out_specs = [pl.BlockSpec((tm, d), lambda i: (i, 0)), pl.BlockSpec((tm, d), lambda i: (i, 0))]
    out_shape = [jax.ShapeDtypeStruct((m, d), F32), jax.ShapeDtypeStruct((m, d), F32 if routed else BF16)]
    if routed:
        in_specs.append(pl.BlockSpec((N_EXPERTS, d), lambda i: (0, 0)))
        args.append(w_router_p)
        out_specs.append(pl.BlockSpec((tm, 128), lambda i: (i, 0)))
        out_shape.append(jax.ShapeDtypeStruct((m, 128), F32))
    return pl.pallas_call(
        kern,
        grid=(m // tm,),
        in_specs=in_specs,
        out_specs=out_specs,
        out_shape=out_shape,
        compiler_params=_cparams(("arbitrary",)),
    )(*args)


def _swiglu_acc(h, wg_ref, wu_ref, wd_ref, acc_ref, tf, widx):
    c0 = 0
    while c0 < tf:
        c1 = min(c0 + 512, tf)
        if widx is None:
            wg, wu, wd = wg_ref[:, c0:c1], wu_ref[:, c0:c1], wd_ref[c0:c1, :]
        else:
            wg, wu, wd = wg_ref[0, :, c0:c1], wu_ref[0, :, c0:c1], wd_ref[0, c0:c1, :]
        g = jnp.dot(h, wg, preferred_element_type=F32)
        u = jnp.dot(h, wu, preferred_element_type=F32)
        a = (g * _sigmoid(g) * u).astype(BF16)
        acc_ref[...] += jnp.dot(a, wd, preferred_element_type=F32)
        c0 = c1


def _ffn_dense_kernel(h_ref, x_ref, wg_ref, wu_ref, wd_ref, mod_ref, lng_ref, lnb_ref, xo_ref, acc_ref,
                      *, tf, alpha):
    f = pl.program_id(1)

    @pl.when(f == 0)
    def _():
        acc_ref[...] = jnp.zeros(acc_ref.shape, F32)

    _swiglu_acc(h_ref[...], wg_ref, wu_ref, wd_ref, acc_ref, tf, None)

    @pl.when(f == pl.num_programs(1) - 1)
    def _():
        xo_ref[...] = _resid_ln(x_ref[...], acc_ref[...], mod_ref[0, 5:6, :], lng_ref[...], lnb_ref[...], alpha)


def _ffn_dense(h, x2d, wg, wu, wd, mod_l, ln_g, ln_b, seq, tm, tf, alpha):
    m, d = x2d.shape
    dff = wg.shape[1]
    tpb = seq // tm
    kern = functools.partial(_ffn_dense_kernel, tf=tf, alpha=alpha)
    return pl.pallas_call(
        kern,
        grid=(m // tm, dff // tf),
        in_specs=[
            pl.BlockSpec((tm, d), lambda i, f: (i, 0)),
            pl.BlockSpec((tm, d), lambda i, f: (i, 0)),
            pl.BlockSpec((d, tf), lambda i, f: (0, f)),
            pl.BlockSpec((d, tf), lambda i, f: (0, f)),
            pl.BlockSpec((tf, d), lambda i, f: (f, 0)),
            pl.BlockSpec((1, 6, d), lambda i, f: (i // tpb, 0, 0)),
            pl.BlockSpec((1, d), lambda i, f: (0, 0)),
            pl.BlockSpec((1, d), lambda i, f: (0, 0)),
        ],
        out_specs=pl.BlockSpec((tm, d), lambda i, f: (i, 0)),
        out_shape=jax.ShapeDtypeStruct((m, d), F32),
        scratch_shapes=[pltpu.VMEM((tm, d), F32)],
        compiler_params=_cparams(("arbitrary", "arbitrary")),
    )(h, x2d, wg, wu, wd, mod_l, ln_g.reshape(1, d), ln_b.reshape(1, d))


def _ffn_grouped_kernel(te_ref, nu_ref, xs_ref, wg_ref, wu_ref, wd_ref, yo_ref, acc_ref, *, tf):
    i = pl.program_id(0)
    f = pl.program_id(1)
    used = i < nu_ref[0]

    @pl.when(jnp.logical_and(used, f == 0))
    def _():
        acc_ref[...] = jnp.zeros(acc_ref.shape, F32)

    @pl.when(used)
    def _():
        _swiglu_acc(xs_ref[...].astype(BF16), wg_ref, wu_ref, wd_ref, acc_ref, tf, 0)

    @pl.when(jnp.logical_and(used, f == pl.num_programs(1) - 1))
    def _():
        yo_ref[...] = acc_ref[...]


def _ffn_grouped(xs, tile_expert, n_used, wg, wu, wd, tm, tf):
    n_rows, d = xs.shape
    dff = wg.shape[2]
    nf = dff // tf

    def w_map(i, f, te, nu):
        return (te[i], 0, jnp.where(i < nu[0], f, nf - 1))

    def wd_map(i, f, te, nu):
        return (te[i], jnp.where(i < nu[0], f, nf - 1), 0)

    kern = functools.partial(_ffn_grouped_kernel, tf=tf)
    return pl.pallas_call(
        kern,
        grid_spec=pltpu.PrefetchScalarGridSpec(
            num_scalar_prefetch=2,
            grid=(n_rows // tm, nf),
            in_specs=[
                pl.BlockSpec((tm, d), lambda i, f, te, nu: (i, 0)),
                pl.BlockSpec((1, d, tf), w_map),
                pl.BlockSpec((1, d, tf), w_map),
                pl.BlockSpec((1, tf, d), wd_map),
            ],
            out_specs=pl.BlockSpec((tm, d), lambda i, f, te, nu: (i, 0)),
            scratch_shapes=[pltpu.VMEM((tm, d), F32)],
        ),
        out_shape=jax.ShapeDtypeStruct((n_rows, d), F32),
        compiler_params=_cparams(("arbitrary", "arbitrary")),
    )(tile_expert, n_used, xs, wg, wu, wd)


def _row_copy(src, dst, sem):
    return pltpu.make_async_copy(src, dst, sem)


ROW_UNROLL = 8


def _scatter_kernel(dest_ref, h_ref, xs_ref, sem, *, ts):
    def start(rb, carry):
        for u in range(ROW_UNROLL):
            r = rb * ROW_UNROLL + u
            for s in range(2):
                _row_copy(h_ref.at[pl.ds(r, 1), :], xs_ref.at[pl.ds(dest_ref[0, 0, 2 * r + s], 1), :],
                          sem).start(priority=s)
        return carry

    lax.fori_loop(0, ts // ROW_UNROLL, start, 0)
    for s in range(2):
        _row_copy(h_ref, xs_ref.at[pl.ds(0, ts), :], sem).wait()


def _scatter_rows(h, dest, n_rows, ts):
    n_tok, d = h.shape
    kern = functools.partial(_scatter_kernel, ts=ts)
    return pl.pallas_call(
        kern,
        grid=(n_tok // ts,),
        in_specs=[
            pl.BlockSpec((1, 1, 2 * ts), lambda i: (i, 0, 0), memory_space=pltpu.SMEM),
            pl.BlockSpec((ts, d), lambda i: (i, 0)),
        ],
        out_specs=pl.BlockSpec(memory_space=pl.ANY),
        out_shape=jax.ShapeDtypeStruct((n_rows, d), F32),
        scratch_shapes=[pltpu.SemaphoreType.DMA(())],
        compiler_params=_cparams(("arbitrary",)),
    )(dest.reshape(n_tok // ts, 1, 2 * ts), h)


def _combine_kernel(dest_ref, gate_ref, x_ref, mod_ref, lng_ref, lnb_ref, yb_ref, xo_ref, r0_ref, r1_ref, sem,
                    *, ts, alpha):
    bufs = (r0_ref, r1_ref)

    def start(rb, carry):
        for u in range(ROW_UNROLL):
            r = rb * ROW_UNROLL + u
            for s in range(2):
                _row_copy(yb_ref.at[pl.ds(dest_ref[0, 0, 2 * r + s], 1), :], bufs[s].at[pl.ds(r, 1), :],
                          sem).start(priority=s)
        return carry

    lax.fori_loop(0, ts // ROW_UNROLL, start, 0)
    for s in range(2):
        _row_copy(yb_ref.at[pl.ds(0, ts), :], bufs[s], sem).wait()

    y = gate_ref[:, 0:1] * r0_ref[...] + gate_ref[:, 1:2] * r1_ref[...]
    xo_ref[...] = _resid_ln(x_ref[...], y, mod_ref[0, 5:6, :], lng_ref[...], lnb_ref[...], alpha)


def _combine(dest, gates, x2d, mod_l, ln_g, ln_b, yb, seq, ts, alpha):
    n_tok, d = x2d.shape
    tpb = seq // ts
    kern = functools.partial(_combine_kernel, ts=ts, alpha=alpha)
    return pl.pallas_call(
        kern,
        grid=(n_tok // ts,),
        in_specs=[
            pl.BlockSpec((1, 1, 2 * ts), lambda i: (i, 0, 0), memory_space=pltpu.SMEM),
            pl.BlockSpec((ts, 2), lambda i: (i, 0)),
            pl.BlockSpec((ts, d), lambda i: (i, 0)),
            pl.BlockSpec((1, 6, d), lambda i: (i // tpb, 0, 0)),
            pl.BlockSpec((1, d), lambda i: (0, 0)),
            pl.BlockSpec((1, d), lambda i: (0, 0)),
            pl.BlockSpec(memory_space=pl.ANY),
        ],
        out_specs=pl.BlockSpec((ts, d), lambda i: (i, 0)),
        out_shape=jax.ShapeDtypeStruct((n_tok, d), F32),
        scratch_shapes=[pltpu.VMEM((ts, d), F32), pltpu.VMEM((ts, d), F32), pltpu.SemaphoreType.DMA(())],
        compiler_params=_cparams(("arbitrary",)),
    )(dest.reshape(n_tok // ts, 1, 2 * ts), gates, x2d, mod_l, ln_g.reshape(1, d), ln_b.reshape(1, d), yb)


def _route(logits, tm_e):
    n_tok = logits.shape[0]
    top_val, top_idx = lax.top_k(logits, 2)
    gates = jax.nn.softmax(top_val, axis=-1)
    e_flat = top_idx.reshape(-1).astype(I32)
    onehot = (e_flat[:, None] == jnp.arange(N_EXPERTS, dtype=I32)[None, :]).astype(I32)
    csum = jnp.cumsum(onehot, axis=0)
    counts = csum[-1]
    rank = jnp.sum(csum * onehot, axis=1) - 1
    padded = ((counts + tm_e - 1) // tm_e) * tm_e
    pend = jnp.cumsum(padded)
    pstart = pend - padded
    dest = (pstart[e_flat] + rank).astype(I32)
    n_tiles = (2 * n_tok) // tm_e + N_EXPERTS
    n_used = (pend[-1] // tm_e).astype(I32)
    tile_start = jnp.arange(n_tiles, dtype=I32) * tm_e
    tile_expert = jnp.minimum(jnp.searchsorted(pend, tile_start, side='right'), N_EXPERTS - 1).astype(I32)
    last_e = tile_expert[jnp.maximum(n_used - 1, 0)]
    tile_expert = jnp.where(jnp.arange(n_tiles) < n_used, tile_expert, last_e)
    return dest, gates, tile_expert, n_used.reshape(1), n_tiles


def _reorder_w_in(w_in_l):
    widths = (512, 64, 64, 512, 64, 8, 256, 256, 512, 512, 16)
    offs = [0]
    for wd_ in widths:
        offs.append(offs[-1] + wd_)
    seg = [w_in_l[:, offs[k]:offs[k + 1]] for k in range(len(widths))]
    aq, ak, av, iq, ik, iw, bq, bk, bv, bg, ba = seg
    d = w_in_l.shape[0]
    z = lambda n: jnp.zeros((d, n), w_in_l.dtype)
    aq = aq * (1.4426950408889634 * A_HEAD_DIM ** -0.5)
    w_r = jnp.concatenate([aq, iq, ak, z(64), ik, z(64), bq, bk, bv, bg, iw, ba, z(104)], axis=1).astype(BF16)
    return w_r, jnp.transpose(av).astype(BF16)


def kernel(x, c, w_in, w_a2, b_a, gla_norm_g, w_out, w_mod, b_mod, ln_g, ln_b, ffn_w_gate, ffn_w_up, ffn_w_down,
           w_router, b_router, moe_w_gate, moe_w_up, moe_w_down):
    bsz, seq, d = x.shape
    depth = w_in.shape[0]
    dff = ffn_w_gate.shape[2]
    m = bsz * seq
    alpha = (2 * depth) ** 0.25

    tm_p = min(512, seq)
    tm_f = min(1024, seq)
    tm_e = min(1024, seq)
    T = min(128, seq)
    tk = min(512, seq)
    R = min(512, seq)
    ts = min(512, seq)
    tf = dff // 2 if (dff // 2) % 128 == 0 else dff

    mod = _modulation(c, w_mod, b_mod).reshape(depth, bsz, 6, d)
    x2d = x.reshape(m, d)
    for l in range(depth):
        mod_l = mod[l]
        w_r, w_vt = _reorder_w_in(w_in[l])
        p1, p2, p3, vt = _proj(x2d, mod_l, w_r, w_vt, seq, tm_p, tk)
        ya = _dsa(p1, p3, vt, bsz, seq, T, tk)
        ob = _gla(p2, p3, w_a2[l], b_a[l], gla_norm_g[l], bsz, seq, R)
        j = l // 2
        if l % 2 == 0:
            x1, h2 = _outproj(ya, ob, x2d, w_out[l].astype(BF16), mod_l, ln_g[l, 0], ln_b[l, 0], None,
                              seq, tm_p, alpha)
            x2d = _ffn_dense(h2, x1, ffn_w_gate[j].astype(BF16), ffn_w_up[j].astype(BF16),
                             ffn_w_down[j].astype(BF16), mod_l, ln_g[l, 1], ln_b[l, 1], seq, tm_f, tf, alpha)
        else:
            wr = jnp.transpose(w_router[j])
            x1, h2, lg = _outproj(ya, ob, x2d, w_out[l].astype(BF16), mod_l, ln_g[l, 0], ln_b[l, 0], wr,
                                  seq, tm_p, alpha)
            logits = lg[:, :N_EXPERTS] + b_router[j][None, :]
            dest, gates, tile_expert, n_used, n_tiles = _route(logits, tm_e)
            xs = _scatter_rows(h2, dest, n_tiles * tm_e, ts)
            yb = _ffn_grouped(xs, tile_expert, n_used, moe_w_gate[j].astype(BF16), moe_w_up[j].astype(BF16),
                              moe_w_down[j].astype(BF16), tm_e, tf)
            x2d = _combine(dest, gates, x1, mod_l, ln_g[l, 1], ln_b[l, 1], yb, seq, ts, alpha)
    return x2d.reshape(bsz, seq, d)
```

```python
import functools

import jax
import jax.numpy as jnp
from jax import lax
from jax.experimental import pallas as pl
from jax.experimental.pallas import tpu as pltpu

F32 = jnp.float32
BF16 = jnp.bfloat16
I32 = jnp.int32
HIGHEST = lax.Precision.HIGHEST

A_HEADS = 8
A_HEAD_DIM = 64
IDX_HEADS = 8
IDX_DIM = 64
TOPK_MAX = 256
B_HEADS = 4
B_KEY_DIM = 64
B_VAL_DIM = 128
GATE_RANK = 16
GATE_TAU = 16.0
GLA_CHUNK = 64
N_EXPERTS = 8
NORM_EPS = 1e-5

N1, N2, N3 = 1280, 1024, 640
IW_OFF = 0
BA_OFF = 8

ONES_ROWS = 16
LANE_GROUP = 256
VMEM_LIMIT = 56 * 1024 * 1024
INT_MIN = -2147483648
NEG_BIG = -1e30


def _sigmoid(x):
    return 1.0 / (1.0 + jnp.exp(-x))


def _cparams(sem):
    return pltpu.CompilerParams(dimension_semantics=sem, vmem_limit_bytes=VMEM_LIMIT)


def _mod_kernel(c_ref, w_ref, b_ref, o_ref):
    c = c_ref[...]
    ca = c * _sigmoid(c)
    o_ref[0] = jnp.dot(ca, w_ref[0], precision=HIGHEST, preferred_element_type=F32) + b_ref[0]


def _modulation(c, w_mod, b_mod):
    depth, d, n = w_mod.shape
    bsz = c.shape[0]
    tn = 1536 if n % 1536 == 0 else n
    return pl.pallas_call(
        _mod_kernel,
        grid=(depth, n // tn),
        in_specs=[
            pl.BlockSpec((bsz, d), lambda l, j: (0, 0)),
            pl.BlockSpec((1, d, tn), lambda l, j: (l, 0, j)),
            pl.BlockSpec((1, 1, tn), lambda l, j: (l, 0, j)),
        ],
        out_specs=pl.BlockSpec((1, bsz, tn), lambda l, j: (l, 0, j)),
        out_shape=jax.ShapeDtypeStruct((depth, bsz, n), F32),
        compiler_params=_cparams(("arbitrary", "arbitrary")),
    )(c, w_mod, b_mod.reshape(depth, 1, n))


def _proj_kernel(x_ref, mod_ref, w_ref, wvt_ref, o1_ref, o2_ref, o3_ref, vt_ref, *, tk):
    x = x_ref[...]
    sh = mod_ref[0, 0:1, :]
    sc = mod_ref[0, 1:2, :]
    h = (x * (1.0 + sc) + sh).astype(BF16)
    o1_ref[...] = jnp.dot(h, w_ref[:, 0:N1], preferred_element_type=F32).astype(BF16)
    o2_ref[...] = jnp.dot(h, w_ref[:, N1:N1 + N2], preferred_element_type=F32).astype(BF16)
    o3_ref[...] = jnp.dot(h, w_ref[:, N1 + N2:N1 + N2 + N3], preferred_element_type=F32)
    nt = (((1,), (1,)), ((), ()))
    for c in range(vt_ref.shape[0]):
        vt_ref[c] = lax.dot_general(wvt_ref[...], h[c * tk:(c + 1) * tk, :], nt,
                                    preferred_element_type=F32).astype(BF16)


def _proj(x2d, mod_l, w_in_r, w_vt, seq, tm, tk):
    m, d = x2d.shape
    tpb = seq // tm
    nw = N1 + N2 + N3
    kern = functools.partial(_proj_kernel, tk=tk)
    return pl.pallas_call(
        kern,
        grid=(m // tm,),
        in_specs=[
            pl.BlockSpec((tm, d), lambda i: (i, 0)),
            pl.BlockSpec((1, 6, d), lambda i: (i // tpb, 0, 0)),
            pl.BlockSpec((d, nw), lambda i: (0, 0)),
            pl.BlockSpec((A_HEAD_DIM, d), lambda i: (0, 0)),
        ],
        out_specs=[
            pl.BlockSpec((tm, N1), lambda i: (i, 0)),
            pl.BlockSpec((tm, N2), lambda i: (i, 0)),
            pl.BlockSpec((tm, N3), lambda i: (i, 0)),
            pl.BlockSpec((tm // tk, A_HEAD_DIM, tk), lambda i: (i, 0, 0)),
        ],
        out_shape=[
            jax.ShapeDtypeStruct((m, N1), BF16),
            jax.ShapeDtypeStruct((m, N2), BF16),
            jax.ShapeDtypeStruct((m, N3), F32),
            jax.ShapeDtypeStruct((m // tk, A_HEAD_DIM, tk), BF16),
        ],
        compiler_params=_cparams(("arbitrary",)),
    )(x2d, mod_l, w_in_r, w_vt)


def _bit_transpose32(words):
    a = list(words)
    mask, j = 0x0000FFFF, 16
    while j:
        k = 0
        while k < 32:
            t = (a[k] ^ lax.shift_right_logical(a[k + j], jnp.int32(j))) & mask
            a[k] = a[k] ^ t
            a[k + j] = a[k + j] ^ lax.shift_left(t, jnp.int32(j))
            k = (k + j + 1) & ~j
        j >>= 1
        mask = (mask ^ (mask << j)) & 0xFFFFFFFF
    return a


def _tree_sum(xs):
    xs = list(xs)
    while len(xs) > 1:
        xs = [xs[i] + xs[i + 1] for i in range(0, len(xs) - 1, 2)] + ([xs[-1]] if len(xs) % 2 else [])
    return xs[0]


def _dsa_kernel(q_ref, kv_ref, vt_ref, w_ref, qext_ref, postab_ref, o_ref,
                kx_ref, vtx_ref, qx_ref, qis_ref, keys_ref, planes_ref, m_ref, acc_ref, *, T, tk, k_top):
    i = pl.program_id(1)
    t0 = i * T
    log_tk = tk.bit_length() - 1
    nk = lax.shift_right_logical(t0 + T + tk - 1, log_tk)
    G = LANE_GROUP // T
    NG = A_HEADS // G
    WG = tk // 256
    nt = (((1,), (1,)), ((), ()))

    @pl.when(i == 0)
    def _():
        kx_ref[...] = kv_ref[:, 0:128] + postab_ref[...]
        qx_ref[:, 64:128] = qext_ref[...]
        vtx_ref[:, 0:A_HEAD_DIM, :] = vt_ref[...]
        vtx_ref[:, A_HEAD_DIM:, :] = jnp.ones((vtx_ref.shape[0], ONES_ROWS, tk), BF16)
        planes_ref[...] = jnp.zeros(planes_ref.shape, I32)

    for h in range(A_HEADS):
        qx_ref[h * T:(h + 1) * T, 0:64] = q_ref[:, h * A_HEAD_DIM:(h + 1) * A_HEAD_DIM]
        qis_ref[h * T:(h + 1) * T, :] = q_ref[:, 512 + h * IDX_DIM:512 + (h + 1) * IDX_DIM]

    row = lax.broadcasted_iota(I32, (tk, T), 0)
    col = lax.broadcasted_iota(I32, (tk, T), 1)
    cmr = col - row

    w_t = jnp.transpose(w_ref[...]) * ((IDX_HEADS ** -0.5) * (IDX_DIM ** -0.5))
    w_rows = [w_t[IW_OFF + h:IW_OFF + h + 1, :] for h in range(IDX_HEADS)]

    def idx_logits(j):
        off = pl.multiple_of(j * tk, tk)
        ki = kv_ref[pl.ds(off, tk), 128:192]
        return [lax.dot_general(ki, qis_ref[g * LANE_GROUP:(g + 1) * LANE_GROUP, :], nt,
                                preferred_element_type=F32) for g in range(NG)]

    def idx_keys(j, s_all):
        acc = jnp.zeros((tk, T), F32)
        for g in range(NG):
            for hh in range(G):
                acc = acc + w_rows[g * G + hh] * jnp.maximum(s_all[g][:, hh * T:(hh + 1) * T], 0.0)
        bits = pltpu.bitcast(acc, I32)
        key = jnp.where(bits >= 0, bits, bits ^ 0x7FFFFFFF)
        valid = (cmr + (t0 - j * tk)) >= 0
        key = jnp.where(valid, key, INT_MIN)
        keys_ref[j] = key.reshape(tk // 8, 8, T)
        u = key ^ INT_MIN
        for wg in range(WG):
            words = [u[(wg * 32 + r) * 8:(wg * 32 + r + 1) * 8, :] for r in range(32)]
            planes = _bit_transpose32(words)
            for q in range(32):
                planes_ref[j, wg, q] = planes[q]

    def p1_pair(jj, carry):
        s_a = idx_logits(2 * jj)
        s_b = idx_logits(2 * jj + 1)
        idx_keys(2 * jj, s_a)
        idx_keys(2 * jj + 1, s_b)
        return carry

    lax.fori_loop(0, lax.shift_right_logical(nk, 1), p1_pair, 0)

    @pl.when((nk & 1) == 1)
    def _():
        idx_keys(nk - 1, idx_logits(nk - 1))

    kf = float(k_top)
    nkt = keys_ref.shape[0]
    alive = [jnp.where(j < nk, jnp.full((8, T), -1, I32), jnp.zeros((8, T), I32))
             for j in range(nkt) for _ in range(WG)]
    cnt_above = jnp.zeros((1, T), F32)
    thr_u = jnp.zeros((1, T), I32)
    for q in range(32):
        ones = [alive[j * WG + wg] & planes_ref[j, wg, q] for j in range(nkt) for wg in range(WG)]
        c1 = jnp.sum(_tree_sum([lax.population_count(o) for o in ones]).astype(F32), axis=0, keepdims=True)
        ok = (cnt_above + c1) >= kf
        ok_b = jnp.broadcast_to(ok, (8, T))
        alive = [jnp.where(ok_b, o, a ^ o) for a, o in zip(alive, ones)]
        cnt_above = jnp.where(ok, cnt_above, cnt_above + c1)
        thr_u = thr_u | jnp.where(ok, jnp.int32(INT_MIN if q == 0 else 1 << (31 - q)), 0)
    n_tie = jnp.sum(_tree_sum([lax.population_count(a) for a in alive]).astype(F32), axis=0, keepdims=True)
    need = kf - cnt_above
    thr = jnp.maximum(thr_u ^ INT_MIN, INT_MIN + 1)

    surplus = jnp.where(thr_u != 0, n_tie - need, 0.0)
    has_ties = jnp.max(surplus) > 0.0

    @pl.when(has_ties)
    def _():
        thr_t = jnp.broadcast_to(thr, (tk, T))
        r2 = lax.broadcasted_iota(I32, (tk, tk), 0)
        c2 = lax.broadcasted_iota(I32, (tk, tk), 1)
        tri = jnp.where(r2 >= c2, 1.0, 0.0).astype(BF16)

        def fix_body(j, base):
            tile = keys_ref[j].reshape(tk, T)
            tie = tile == thr_t
            pref = jnp.dot(tri, jnp.where(tie, 1.0, 0.0).astype(BF16), preferred_element_type=F32) + base
            drop = jnp.logical_and(tie, pref > need)
            keys_ref[j] = jnp.where(drop, tile - 1, tile).reshape(tk // 8, 8, T)
            return pref[tk - 1:tk, :]

        lax.fori_loop(0, nk, fix_body, jnp.zeros((1, T), F32))

    m_ref[...] = jnp.full(m_ref.shape, NEG_BIG, F32)
    acc_ref[...] = jnp.zeros(acc_ref.shape, F32)
    thr_b = jnp.broadcast_to(thr, (tk, T))

    def tile_logits(j):
        off = pl.multiple_of(j * tk, tk)
        kx = kx_ref[pl.ds(off, tk), :]
        return [lax.dot_general(kx, qx_ref[g * LANE_GROUP:(g + 1) * LANE_GROUP, :], nt,
                                preferred_element_type=F32) for g in range(NG)]

    def tile_softmax(j, s_all):
        vtx = vtx_ref[j]
        sel = keys_ref[j].reshape(tk, T) >= thr_b
        sel_g = jnp.concatenate([sel] * G, axis=1) if G > 1 else sel
        for g in range(NG):
            s = jnp.where(sel_g, s_all[g], NEG_BIG)
            m_old = m_ref[g]
            m_new = jnp.maximum(m_old, jnp.max(s, axis=0, keepdims=True))
            p = jnp.exp2(s - m_new).astype(BF16)
            acc_ref[g] = jnp.exp2(m_old - m_new) * acc_ref[g] + jnp.dot(vtx, p, preferred_element_type=F32)
            m_ref[g] = m_new

    def p3_pair(jj, carry):
        s_a = tile_logits(2 * jj)
        s_b = tile_logits(2 * jj + 1)
        tile_softmax(2 * jj, s_a)
        tile_softmax(2 * jj + 1, s_b)
        return carry

    lax.fori_loop(0, lax.shift_right_logical(nk, 1), p3_pair, 0)

    @pl.when((nk & 1) == 1)
    def _():
        tile_softmax(nk - 1, tile_logits(nk - 1))

    def head_out(h):
        g, hh = h // G, h % G
        a = acc_ref[g]
        return a[0:A_HEAD_DIM, hh * T:(hh + 1) * T] / a[A_HEAD_DIM:A_HEAD_DIM + 1, hh * T:(hh + 1) * T]

    for h in range(0, A_HEADS, 2):
        pair = jnp.concatenate([head_out(h), head_out(h + 1)], axis=0)
        o_ref[:, h * A_HEAD_DIM:(h + 2) * A_HEAD_DIM] = jnp.transpose(pair).astype(BF16)


def _alibi_tables(seq, T):
    log2e = 1.4426950408889634
    pos = jnp.arange(seq, dtype=I32)
    hi = (pos // 64).astype(F32)
    lo = (pos % 64).astype(F32)
    z = jnp.zeros((seq,), F32)
    pos_cols = [z] * 64 + [hi, hi, hi, lo, lo, lo] + [z] * 58
    postab = jnp.stack(pos_cols, axis=1).astype(BF16)
    rows = []
    for h in range(A_HEADS):
        c = jnp.asarray(2.0 ** (-8.0 * (h + 1) / A_HEADS) * log2e, F32)
        c1 = c.astype(BF16)
        c2 = (c - c1.astype(F32)).astype(BF16)
        c3 = (c - c1.astype(F32) - c2.astype(F32)).astype(BF16)
        r = jnp.zeros((64,), BF16)
        r = r.at[0].set(c1 * 64).at[1].set(c2 * 64).at[2].set(c3 * 64).at[3].set(c1).at[4].set(c2).at[5].set(c3)
        rows.append(jnp.broadcast_to(r[None, :], (T, 64)))
    qext = jnp.concatenate(rows, axis=0)
    return postab, qext


def _dsa(p1, p3, vt, bsz, seq, T, tk):
    m = p1.shape[0]
    nq = seq // T
    nkt = seq // tk
    k_top = min(TOPK_MAX, seq // 4)
    postab, qext = _alibi_tables(seq, T)
    kern = functools.partial(_dsa_kernel, T=T, tk=tk, k_top=k_top)
    ng = A_HEADS * T // LANE_GROUP
    return pl.pallas_call(
        kern,
        grid=(bsz, nq),
        in_specs=[
            pl.BlockSpec((T, 1024), lambda b, i: (b * nq + i, 0)),
            pl.BlockSpec((seq, 256), lambda b, i: (b, 4)),
            pl.BlockSpec((nkt, A_HEAD_DIM, tk), lambda b, i: (b, 0, 0)),
            pl.BlockSpec((T, 128), lambda b, i: (b * nq + i, 4)),
            pl.BlockSpec((A_HEADS * T, 64), lambda b, i: (0, 0)),
            pl.BlockSpec((seq, 128), lambda b, i: (0, 0)),
        ],
        out_specs=pl.BlockSpec((T, 512), lambda b, i: (b * nq + i, 0)),
        out_shape=jax.ShapeDtypeStruct((m, 512), BF16),
        scratch_shapes=[
            pltpu.VMEM((seq, 128), BF16),
            pltpu.VMEM((nkt, A_HEAD_DIM + ONES_ROWS, tk), BF16),
            pltpu.VMEM((A_HEADS * T, 128), BF16),
            pltpu.VMEM((A_HEADS * T, IDX_DIM), BF16),
            pltpu.VMEM((nkt, tk // 8, 8, T), I32),
            pltpu.VMEM((nkt, tk // 256, 32, 8, T), I32),
            pltpu.VMEM((ng, 1, LANE_GROUP), F32),
            pltpu.VMEM((ng, A_HEAD_DIM + ONES_ROWS, LANE_GROUP), F32),
        ],
        compiler_params=_cparams(("arbitrary", "arbitrary")),
    )(p1, p1, vt, p3, qext, postab)


def _gla_kernel(p2_ref, p3_ref, wa2_ref, ba_ref, g_ref, o_ref, st_ref, *, R):
    C = GLA_CHUNK

    @pl.when(pl.program_id(1) == 0)
    def _():
        st_ref[...] = jnp.zeros(st_ref.shape, F32)

    ba = p3_ref[:, 512 + BA_OFF:512 + BA_OFF + GATE_RANK]
    xg = jnp.dot(ba, wa2_ref[...], precision=HIGHEST, preferred_element_type=F32) + ba_ref[...]
    log_a = (jnp.minimum(xg, 0.0) - jnp.log(1.0 + jnp.exp(-jnp.abs(xg)))) * (1.0 / GATE_TAU)

    r_i = lax.broadcasted_iota(I32, (C, C), 0)
    c_i = lax.broadcasted_iota(I32, (C, C), 1)
    tri_b = r_i >= c_i
    tri_f = tri_b.astype(F32)
    nt = (((1,), (1,)), ((), ()))
    tn = (((0,), (0,)), ((), ()))
    gamma = g_ref[...]

    sts = [st_ref[h] for h in range(B_HEADS)]
    heads = range(B_HEADS)
    kss = [slice(h * B_KEY_DIM, (h + 1) * B_KEY_DIM) for h in heads]
    bs = [jnp.dot(tri_f, log_a[c * C:(c + 1) * C, :], precision=HIGHEST, preferred_element_type=F32)
          for c in range(R // C)]
    for c in range(R // C):
        rows = slice(c * C, (c + 1) * C)
        b = bs[c]
        b_last = b[C - 1:C, :]
        b_mid = b[C // 2 - 1:C // 2, :]
        q = p2_ref[rows, 0:256].astype(F32) * (B_KEY_DIM ** -0.5)
        k = p2_ref[rows, 256:512].astype(F32)
        qe = (q * jnp.exp(b)).astype(BF16)
        qm = (q * jnp.exp(b - b_mid)).astype(BF16)
        km = (k * jnp.exp(b_mid - b)).astype(BF16)
        kl = (k * jnp.exp(b_last - b)).astype(BF16)
        dec = jnp.exp(b_last)
        vs_ = [p2_ref[rows, 512 + h * B_VAL_DIM:512 + (h + 1) * B_VAL_DIM] for h in heads]
        a_s = [lax.dot_general(qm[:, kss[h]], km[:, kss[h]], nt, preferred_element_type=F32) for h in heads]
        o_s = [lax.dot_general(qe[:, kss[h]], sts[h].astype(BF16), nt, preferred_element_type=F32) for h in heads]
        x_s = [lax.dot_general(vs_[h], kl[:, kss[h]], tn, preferred_element_type=F32) for h in heads]
        a_s = [jnp.where(tri_b, a, 0.0).astype(BF16) for a in a_s]
        o_s = [o_s[h] + jnp.dot(a_s[h], vs_[h], preferred_element_type=F32) for h in heads]
        sts = [sts[h] * dec[:, kss[h]] + x_s[h] for h in heads]
        for h in heads:
            o = o_s[h]
            ms = jnp.mean(o * o, axis=1, keepdims=True)
            on = o * lax.rsqrt(ms + NORM_EPS) * gamma
            gate = p3_ref[rows, h * B_VAL_DIM:(h + 1) * B_VAL_DIM]
            o_ref[rows, h * B_VAL_DIM:(h + 1) * B_VAL_DIM] = (on * (gate * _sigmoid(gate))).astype(BF16)
    for h in range(B_HEADS):
        st_ref[h] = sts[h]


def _gla(p2, p3, w_a2, b_a, gamma, bsz, seq, R):
    m = p2.shape[0]
    nr = seq // R
    kern = functools.partial(_gla_kernel, R=R)
    return pl.pallas_call(
        kern,
        grid=(bsz, nr),
        in_specs=[
            pl.BlockSpec((R, N2), lambda b, i: (b * nr + i, 0)),
            pl.BlockSpec((R, N3), lambda b, i: (b * nr + i, 0)),
            pl.BlockSpec((GATE_RANK, 256), lambda b, i: (0, 0)),
            pl.BlockSpec((1, 256), lambda b, i: (0, 0)),
            pl.BlockSpec((1, B_VAL_DIM), lambda b, i: (0, 0)),
        ],
        out_specs=pl.BlockSpec((R, 512), lambda b, i: (b * nr + i, 0)),
        out_shape=jax.ShapeDtypeStruct((m, 512), BF16),
        scratch_shapes=[pltpu.VMEM((B_HEADS, B_VAL_DIM, B_KEY_DIM), F32)],
        compiler_params=_cparams(("arbitrary", "arbitrary")),
    )(p2, p3, w_a2, b_a.reshape(1, 256), gamma.reshape(1, B_VAL_DIM))


def _resid_ln(x, y, gate, ln_g, ln_b, alpha):
    z = alpha * x + (1.0 + gate) * y
    mu = jnp.mean(z, axis=1, keepdims=True)
    zc = z - mu
    var = jnp.mean(zc * zc, axis=1, keepdims=True)
    return zc * lax.rsqrt(var + NORM_EPS) * ln_g + ln_b


def _outproj_kernel(ya_ref, ob_ref, x_ref, w_ref, mod_ref, lng_ref, lnb_ref, *rest, alpha, routed):
    if routed:
        wr_ref, xo_ref, h_ref, lg_ref = rest
    else:
        xo_ref, h_ref = rest
    y = jnp.dot(ya_ref[...], w_ref[0:512, :], preferred_element_type=F32)
    y = y + jnp.dot(ob_ref[...], w_ref[512:1024, :], preferred_element_type=F32)
    xn = _resid_ln(x_ref[...], y, mod_ref[0, 2:3, :], lng_ref[...], lnb_ref[...], alpha)
    xo_ref[...] = xn
    h = xn * (1.0 + mod_ref[0, 4:5, :]) + mod_ref[0, 3:4, :]
    if routed:
        h_ref[...] = h
        lane = lax.broadcasted_iota(I32, (1, 128), 1)
        lg = jnp.zeros((h.shape[0], 128), F32)
        for e in range(N_EXPERTS):
            col = jnp.sum(h * wr_ref[e:e + 1, :], axis=1, keepdims=True)
            lg = lg + col * jnp.where(lane == e, 1.0, 0.0)
        lg_ref[...] = lg
    else:
        h_ref[...] = h.astype(BF16)


def _outproj(ya, ob, x2d, w_out, mod_l, ln_g, ln_b, w_router_p, seq, tm, alpha):
    m, d = x2d.shape
    tpb = seq // tm
    routed = w_router_p is not None
    kern = functools.partial(_outproj_kernel, alpha=alpha, routed=routed)
    in_specs = [
        pl.BlockSpec((tm, 512), lambda i: (i, 0)),
        pl.BlockSpec((tm, 512), lambda i: (i, 0)),
        pl.BlockSpec((tm, d), lambda i: (i, 0)),
        pl.BlockSpec((d, d), lambda i: (0, 0)),
        pl.BlockSpec((1, 6, d), lambda i: (i // tpb, 0, 0)),
        pl.BlockSpec((1, d), lambda i: (0, 0)),
        pl.BlockSpec((1, d), lambda i: (0, 0)),
    ]
    args = [ya, ob, x2d, w_out, mod_l, ln_g.reshape(1, d), ln_b.reshape(1, d)]
    out_specs = [pl.BlockSpec((tm, d), lambda i: (i, 0)), pl.BlockSpec((tm, d), lambda i: (i, 0))]
    out_shape = [jax.ShapeDtypeStruct((m, d), F32), jax.ShapeDtypeStruct((m, d), F32 if routed else BF16)]
    if routed:
        in_specs.append(pl.BlockSpec((N_EXPERTS, d), lambda i: (0, 0)))
        args.append(w_router_p)
        out_specs.append(pl.BlockSpec((tm, 128), lambda i: (i, 0)))
        out_shape.append(jax.ShapeDtypeStruct((m, 128), F32))
    return pl.pallas_call(
        kern,
        grid=(m // tm,),
        in_specs=in_specs,
        out_specs=out_specs,
        out_shape=out_shape,
        compiler_params=_cparams(("arbitrary",)),
    )(*args)


def _swiglu_acc(h, wg_ref, wu_ref, wd_ref, acc_ref, tf, widx):
    c0 = 0
    while c0 < tf:
        c1 = min(c0 + 512, tf)
        if widx is None:
            wg, wu, wd = wg_ref[:, c0:c1], wu_ref[:, c0:c1], wd_ref[c0:c1, :]
        else:
            wg, wu, wd = wg_ref[0, :, c0:c1], wu_ref[0, :, c0:c1], wd_ref[0, c0:c1, :]
        g = jnp.dot(h, wg, preferred_element_type=F32)
        u = jnp.dot(h, wu, preferred_element_type=F32)
        a = (g * _sigmoid(g) * u).astype(BF16)
        acc_ref[...] += jnp.dot(a, wd, preferred_element_type=F32)
        c0 = c1


def _ffn_dense_kernel(h_ref, x_ref, wg_ref, wu_ref, wd_ref, mod_ref, lng_ref, lnb_ref, xo_ref, acc_ref,
                      *, tf, alpha):
    f = pl.program_id(1)

    @pl.when(f == 0)
    def _():
        acc_ref[...] = jnp.zeros(acc_ref.shape, F32)

    _swiglu_acc(h_ref[...], wg_ref, wu_ref, wd_ref, acc_ref, tf, None)

    @pl.when(f == pl.num_programs(1) - 1)
    def _():
        xo_ref[...] = _resid_ln(x_ref[...], acc_ref[...], mod_ref[0, 5:6, :], lng_ref[...], lnb_ref[...], alpha)


def _ffn_dense(h, x2d, wg, wu, wd, mod_l, ln_g, ln_b, seq, tm, tf, alpha):
    m, d = x2d.shape
    dff = wg.shape[1]
    tpb = seq // tm
    kern = functools.partial(_ffn_dense_kernel, tf=tf, alpha=alpha)
    return pl.pallas_call(
        kern,
        grid=(m // tm, dff // tf),
        in_specs=[
            pl.BlockSpec((tm, d), lambda i, f: (i, 0)),
            pl.BlockSpec((tm, d), lambda i, f: (i, 0)),
            pl.BlockSpec((d, tf), lambda i, f: (0, f)),
            pl.BlockSpec((d, tf), lambda i, f: (0, f)),
            pl.BlockSpec((tf, d), lambda i, f: (f, 0)),
            pl.BlockSpec((1, 6, d), lambda i, f: (i // tpb, 0, 0)),
            pl.BlockSpec((1, d), lambda i, f: (0, 0)),
            pl.BlockSpec((1, d), lambda i, f: (0, 0)),
        ],
        out_specs=pl.BlockSpec((tm, d), lambda i, f: (i, 0)),
        out_shape=jax.ShapeDtypeStruct((m, d), F32),
        scratch_shapes=[pltpu.VMEM((tm, d), F32)],
        compiler_params=_cparams(("arbitrary", "arbitrary")),
    )(h, x2d, wg, wu, wd, mod_l, ln_g.reshape(1, d), ln_b.reshape(1, d))


def _ffn_grouped_kernel(te_ref, nu_ref, xs_ref, wg_ref, wu_ref, wd_ref, yo_ref, acc_ref, *, tf):
    i = pl.program_id(0)
    f = pl.program_id(1)
    used = i < nu_ref[0]

    @pl.when(jnp.logical_and(used, f == 0))
    def _():
        acc_ref[...] = jnp.zeros(acc_ref.shape, F32)

    @pl.when(used)
    def _():
        _swiglu_acc(xs_ref[...].astype(BF16), wg_ref, wu_ref, wd_ref, acc_ref, tf, 0)

    @pl.when(jnp.logical_and(used, f == pl.num_programs(1) - 1))
    def _():
        yo_ref[...] = acc_ref[...]


def _ffn_grouped(xs, tile_expert, n_used, wg, wu, wd, tm, tf):
    n_rows, d = xs.shape
    dff = wg.shape[2]
    nf = dff // tf

    def w_map(i, f, te, nu):
        return (te[i], 0, jnp.where(i < nu[0], f, nf - 1))

    def wd_map(i, f, te, nu):
        return (te[i], jnp.where(i < nu[0], f, nf - 1), 0)

    kern = functools.partial(_ffn_grouped_kernel, tf=tf)
    return pl.pallas_call(
        kern,
        grid_spec=pltpu.PrefetchScalarGridSpec(
            num_scalar_prefetch=2,
            grid=(n_rows // tm, nf),
            in_specs=[
                pl.BlockSpec((tm, d), lambda i, f, te, nu: (i, 0)),
                pl.BlockSpec((1, d, tf), w_map),
                pl.BlockSpec((1, d, tf), w_map),
                pl.BlockSpec((1, tf, d), wd_map),
            ],
            out_specs=pl.BlockSpec((tm, d), lambda i, f, te, nu: (i, 0)),
            scratch_shapes=[pltpu.VMEM((tm, d), F32)],
        ),
        out_shape=jax.ShapeDtypeStruct((n_rows, d), F32),
        compiler_params=_cparams(("arbitrary", "arbitrary")),
    )(tile_expert, n_used, xs, wg, wu, wd)


def _row_copy(src, dst, sem):
    return pltpu.make_async_copy(src, dst, sem)


ROW_UNROLL = 16


def _scatter_kernel(dest_ref, h_ref, xs_ref, sem, *, ts):
    def start(rb, carry):
        for u in range(ROW_UNROLL):
            r = rb * ROW_UNROLL + u
            for s in range(2):
                _row_copy(h_ref.at[pl.ds(r, 1), :], xs_ref.at[pl.ds(dest_ref[0, 0, 2 * r + s], 1), :],
                          sem).start(priority=s)
        return carry

    lax.fori_loop(0, ts // ROW_UNROLL, start, 0)
    for s in range(2):
        _row_copy(h_ref, xs_ref.at[pl.ds(0, ts), :], sem).wait()


def _scatter_rows(h, dest, n_rows, ts):
    n_tok, d = h.shape
    kern = functools.partial(_scatter_kernel, ts=ts)
    return pl.pallas_call(
        kern,
        grid=(n_tok // ts,),
        in_specs=[
            pl.BlockSpec((1, 1, 2 * ts), lambda i: (i, 0, 0), memory_space=pltpu.SMEM),
            pl.BlockSpec((ts, d), lambda i: (i, 0)),
        ],
        out_specs=pl.BlockSpec(memory_space=pl.ANY),
        out_shape=jax.ShapeDtypeStruct((n_rows, d), F32),
        scratch_shapes=[pltpu.SemaphoreType.DMA(())],
        compiler_params=_cparams(("arbitrary",)),
    )(dest.reshape(n_tok // ts, 1, 2 * ts), h)


def _combine_kernel(dest_ref, gate_ref, x_ref, mod_ref, lng_ref, lnb_ref, yb_ref, xo_ref, r0_ref, r1_ref, sem,
                    *, ts, alpha):
    bufs = (r0_ref, r1_ref)

    def start(rb, carry):
        for u in range(ROW_UNROLL):
            r = rb * ROW_UNROLL + u
            for s in range(2):
                _row_copy(yb_ref.at[pl.ds(dest_ref[0, 0, 2 * r + s], 1), :], bufs[s].at[pl.ds(r, 1), :],
                          sem).start(priority=s)
        return carry

    lax.fori_loop(0, ts // ROW_UNROLL, start, 0)
    for s in range(2):
        _row_copy(yb_ref.at[pl.ds(0, ts), :], bufs[s], sem).wait()

    y = gate_ref[:, 0:1] * r0_ref[...] + gate_ref[:, 1:2] * r1_ref[...]
    xo_ref[...] = _resid_ln(x_ref[...], y, mod_ref[0, 5:6, :], lng_ref[...], lnb_ref[...], alpha)


def _combine(dest, gates, x2d, mod_l, ln_g, ln_b, yb, seq, ts, alpha):
    n_tok, d = x2d.shape
    tpb = seq // ts
    kern = functools.partial(_combine_kernel, ts=ts, alpha=alpha)
    return pl.pallas_call(
        kern,
        grid=(n_tok // ts,),
        in_specs=[
            pl.BlockSpec((1, 1, 2 * ts), lambda i: (i, 0, 0), memory_space=pltpu.SMEM),
            pl.BlockSpec((ts, 2), lambda i: (i, 0)),
            pl.BlockSpec((ts, d), lambda i: (i, 0)),
            pl.BlockSpec((1, 6, d), lambda i: (i // tpb, 0, 0)),
            pl.BlockSpec((1, d), lambda i: (0, 0)),
            pl.BlockSpec((1, d), lambda i: (0, 0)),
            pl.BlockSpec(memory_space=pl.ANY),
        ],
        out_specs=pl.BlockSpec((ts, d), lambda i: (i, 0)),
        out_shape=jax.ShapeDtypeStruct((n_tok, d), F32),
        scratch_shapes=[pltpu.VMEM((ts, d), F32), pltpu.VMEM((ts, d), F32), pltpu.SemaphoreType.DMA(())],
        compiler_params=_cparams(("arbitrary",)),
    )(dest.reshape(n_tok // ts, 1, 2 * ts), gates, x2d, mod_l, ln_g.reshape(1, d), ln_b.reshape(1, d), yb)


def _route(logits, tm_e):
    n_tok = logits.shape[0]
    top_val, top_idx = lax.top_k(logits, 2)
    gates = jax.nn.softmax(top_val, axis=-1)
    e_flat = top_idx.reshape(-1).astype(I32)
    onehot = (e_flat[:, None] == jnp.arange(N_EXPERTS, dtype=I32)[None, :]).astype(I32)
    csum = jnp.cumsum(onehot, axis=0)
    counts = csum[-1]
    rank = jnp.sum(csum * onehot, axis=1) - 1
    padded = ((counts + tm_e - 1) // tm_e) * tm_e
    pend = jnp.cumsum(padded)
    pstart = pend - padded
    dest = (pstart[e_flat] + rank).astype(I32)
    n_tiles = (2 * n_tok) // tm_e + N_EXPERTS
    n_used = (pend[-1] // tm_e).astype(I32)
    tile_start = jnp.arange(n_tiles, dtype=I32) * tm_e
    tile_expert = jnp.minimum(jnp.searchsorted(pend, tile_start, side='right'), N_EXPERTS - 1).astype(I32)
    last_e = tile_expert[jnp.maximum(n_used - 1, 0)]
    tile_expert = jnp.where(jnp.arange(n_tiles) < n_used, tile_expert, last_e)
    return dest, gates, tile_expert, n_used.reshape(1), n_tiles


def _reorder_w_in(w_in_l):
    widths = (512, 64, 64, 512, 64, 8, 256, 256, 512, 512, 16)
    offs = [0]
    for wd_ in widths:
        offs.append(offs[-1] + wd_)
    seg = [w_in_l[:, offs[k]:offs[k + 1]] for k in range(len(widths))]
    aq, ak, av, iq, ik, iw, bq, bk, bv, bg, ba = seg
    d = w_in_l.shape[0]
    z = lambda n: jnp.zeros((d, n), w_in_l.dtype)
    aq = aq * (1.4426950408889634 * A_HEAD_DIM ** -0.5)
    w_r = jnp.concatenate([aq, iq, ak, z(64), ik, z(64), bq, bk, bv, bg, iw, ba, z(104)], axis=1).astype(BF16)
    return w_r, jnp.transpose(av).astype(BF16)


def kernel(x, c, w_in, w_a2, b_a, gla_norm_g, w_out, w_mod, b_mod, ln_g, ln_b, ffn_w_gate, ffn_w_up, ffn_w_down,
           w_router, b_router, moe_w_gate, moe_w_up, moe_w_down):
    bsz, seq, d = x.shape
    depth = w_in.shape[0]
    dff = ffn_w_gate.shape[2]
    m = bsz * seq
    alpha = (2 * depth) ** 0.25

    tm_p = min(512, seq)
    tm_f = min(1024, seq)
    tm_e = min(1024, seq)
    T = min(128, seq)
    tk = min(512, seq)
    R = min(512, seq)
    ts = min(512, seq)
    tf = dff // 2 if (dff // 2) % 128 == 0 else dff

    mod = _modulation(c, w_mod, b_mod).reshape(depth, bsz, 6, d)
    x2d = x.reshape(m, d)
    for l in range(depth):
        mod_l = mod[l]
        w_r, w_vt = _reorder_w_in(w_in[l])
        p1, p2, p3, vt = _proj(x2d, mod_l, w_r, w_vt, seq, tm_p, tk)
        ya = _dsa(p1, p3, vt, bsz, seq, T, tk)
        ob = _gla(p2, p3, w_a2[l], b_a[l], gla_norm_g[l], bsz, seq, R)
        j = l // 2
        if l % 2 == 0:
            x1, h2 = _outproj(ya, ob, x2d, w_out[l].astype(BF16), mod_l, ln_g[l, 0], ln_b[l, 0], None,
                              seq, tm_p, alpha)
            x2d = _ffn_dense(h2, x1, ffn_w_gate[j].astype(BF16), ffn_w_up[j].astype(BF16),
                             ffn_w_down[j].astype(BF16), mod_l, ln_g[l, 1], ln_b[l, 1], seq, tm_f, tf, alpha)
        else:
            wr = jnp.transpose(w_router[j])
            x1, h2, lg = _outproj(ya, ob, x2d, w_out[l].astype(BF16), mod_l, ln_g[l, 0], ln_b[l, 0], wr,
                                  seq, tm_p, alpha)
            logits = lg[:, :N_EXPERTS] + b_router[j][None, :]
            dest, gates, tile_expert, n_used, n_tiles = _route(logits, tm_e)
            xs = _scatter_rows(h2, dest, n_tiles * tm_e, ts)
            yb = _ffn_grouped(xs, tile_expert, n_used, moe_w_gate[j].astype(BF16), moe_w_up[j].astype(BF16),
                              moe_w_down[j].astype(BF16), tm_e, tf)
            x2d = _combine(dest, gates, x1, mod_l, ln_g[l, 1], ln_b[l, 1], yb, seq, ts, alpha)
    return x2d.reshape(bsz, seq, d)
```

```python
import functools

import jax
import jax.numpy as jnp
from jax import lax
from jax.experimental import pallas as pl
from jax.experimental.pallas import tpu as pltpu

F32 = jnp.float32
BF16 = jnp.bfloat16
I32 = jnp.int32
HIGHEST = lax.Precision.HIGHEST

A_HEADS = 8
A_HEAD_DIM = 64
IDX_HEADS = 8
IDX_DIM = 64
TOPK_MAX = 256
B_HEADS = 4
B_KEY_DIM = 64
B_VAL_DIM = 128
GATE_RANK = 16
GATE_TAU = 16.0
GLA_CHUNK = 64
N_EXPERTS = 8
NORM_EPS = 1e-5

N1, N2, N3 = 1280, 1024, 640
IW_OFF = 0
BA_OFF = 8

ONES_ROWS = 16
LANE_GROUP = 256
VMEM_LIMIT = 56 * 1024 * 1024
INT_MIN = -2147483648
NEG_BIG = -1e30


def _sigmoid(x):
    return 1.0 / (1.0 + jnp.exp(-x))


def _cparams(sem):
    return pltpu.CompilerParams(dimension_semantics=sem, vmem_limit_bytes=VMEM_LIMIT)


def _mod_kernel(c_ref, w_ref, b_ref, o_ref):
    c = c_ref[...]
    ca = c * _sigmoid(c)
    o_ref[0] = jnp.dot(ca, w_ref[0], precision=HIGHEST, preferred_element_type=F32) + b_ref[0]


def _modulation(c, w_mod, b_mod):
    depth, d, n = w_mod.shape
    bsz = c.shape[0]
    tn = 1536 if n % 1536 == 0 else n
    return pl.pallas_call(
        _mod_kernel,
        grid=(depth, n // tn),
        in_specs=[
            pl.BlockSpec((bsz, d), lambda l, j: (0, 0)),
            pl.BlockSpec((1, d, tn), lambda l, j: (l, 0, j)),
            pl.BlockSpec((1, 1, tn), lambda l, j: (l, 0, j)),
        ],
        out_specs=pl.BlockSpec((1, bsz, tn), lambda l, j: (l, 0, j)),
        out_shape=jax.ShapeDtypeStruct((depth, bsz, n), F32),
        compiler_params=_cparams(("arbitrary", "arbitrary")),
    )(c, w_mod, b_mod.reshape(depth, 1, n))


def _proj_kernel(x_ref, mod_ref, w_ref, wvt_ref, o1_ref, o2_ref, o3_ref, vt_ref, *, tk):
    x = x_ref[...]
    sh = mod_ref[0, 0:1, :]
    sc = mod_ref[0, 1:2, :]
    h = (x * (1.0 + sc) + sh).astype(BF16)
    o1_ref[...] = jnp.dot(h, w_ref[:, 0:N1], preferred_element_type=F32).astype(BF16)
    o2_ref[...] = jnp.dot(h, w_ref[:, N1:N1 + N2], preferred_element_type=F32).astype(BF16)
    o3_ref[...] = jnp.dot(h, w_ref[:, N1 + N2:N1 + N2 + N3], preferred_element_type=F32)
    nt = (((1,), (1,)), ((), ()))
    for c in range(vt_ref.shape[0]):
        vt_ref[c] = lax.dot_general(wvt_ref[...], h[c * tk:(c + 1) * tk, :], nt,
                                    preferred_element_type=F32).astype(BF16)


def _proj(x2d, mod_l, w_in_r, w_vt, seq, tm, tk):
    m, d = x2d.shape
    tpb = seq // tm
    nw = N1 + N2 + N3
    kern = functools.partial(_proj_kernel, tk=tk)
    return pl.pallas_call(
        kern,
        grid=(m // tm,),
        in_specs=[
            pl.BlockSpec((tm, d), lambda i: (i, 0)),
            pl.BlockSpec((1, 6, d), lambda i: (i // tpb, 0, 0)),
            pl.BlockSpec((d, nw), lambda i: (0, 0)),
            pl.BlockSpec((A_HEAD_DIM, d), lambda i: (0, 0)),
        ],
        out_specs=[
            pl.BlockSpec((tm, N1), lambda i: (i, 0)),
            pl.BlockSpec((tm, N2), lambda i: (i, 0)),
            pl.BlockSpec((tm, N3), lambda i: (i, 0)),
            pl.BlockSpec((tm // tk, A_HEAD_DIM, tk), lambda i: (i, 0, 0)),
        ],
        out_shape=[
            jax.ShapeDtypeStruct((m, N1), BF16),
            jax.ShapeDtypeStruct((m, N2), BF16),
            jax.ShapeDtypeStruct((m, N3), F32),
            jax.ShapeDtypeStruct((m // tk, A_HEAD_DIM, tk), BF16),
        ],
        compiler_params=_cparams(("arbitrary",)),
    )(x2d, mod_l, w_in_r, w_vt)


def _bit_transpose32(words):
    a = list(words)
    mask, j = 0x0000FFFF, 16
    while j:
        k = 0
        while k < 32:
            t = (a[k] ^ lax.shift_right_logical(a[k + j], jnp.int32(j))) & mask
            a[k] = a[k] ^ t
            a[k + j] = a[k + j] ^ lax.shift_left(t, jnp.int32(j))
            k = (k + j + 1) & ~j
        j >>= 1
        mask = (mask ^ (mask << j)) & 0xFFFFFFFF
    return a


def _tree_sum(xs):
    xs = list(xs)
    while len(xs) > 1:
        xs = [xs[i] + xs[i + 1] for i in range(0, len(xs) - 1, 2)] + ([xs[-1]] if len(xs) % 2 else [])
    return xs[0]


def _dsa_kernel(q_ref, kv_ref, vt_ref, w_ref, qext_ref, postab_ref, o_ref,
                kx_ref, vtx_ref, qx_ref, qis_ref, keys_ref, planes_ref, m_ref, acc_ref, *, T, tk, k_top):
    i = pl.program_id(1)
    t0 = i * T
    log_tk = tk.bit_length() - 1
    nk = lax.shift_right_logical(t0 + T + tk - 1, log_tk)
    G = LANE_GROUP // T
    NG = A_HEADS // G
    WG = tk // 256
    nt = (((1,), (1,)), ((), ()))

    @pl.when(i == 0)
    def _():
        kx_ref[...] = kv_ref[:, 0:128] + postab_ref[...]
        qx_ref[:, 64:128] = qext_ref[...]
        vtx_ref[:, 0:A_HEAD_DIM, :] = vt_ref[...]
        vtx_ref[:, A_HEAD_DIM:, :] = jnp.ones((vtx_ref.shape[0], ONES_ROWS, tk), BF16)
        planes_ref[...] = jnp.zeros(planes_ref.shape, I32)

    for h in range(A_HEADS):
        qx_ref[h * T:(h + 1) * T, 0:64] = q_ref[:, h * A_HEAD_DIM:(h + 1) * A_HEAD_DIM]
        qis_ref[h * T:(h + 1) * T, :] = q_ref[:, 512 + h * IDX_DIM:512 + (h + 1) * IDX_DIM]

    row = lax.broadcasted_iota(I32, (tk, T), 0)
    col = lax.broadcasted_iota(I32, (tk, T), 1)
    cmr = col - row

    w_t = jnp.transpose(w_ref[...]) * ((IDX_HEADS ** -0.5) * (IDX_DIM ** -0.5))
    w_rows = [w_t[IW_OFF + h:IW_OFF + h + 1, :] for h in range(IDX_HEADS)]

    def idx_logits(j):
        off = pl.multiple_of(j * tk, tk)
        ki = kv_ref[pl.ds(off, tk), 128:192]
        return [lax.dot_general(ki, qis_ref[g * LANE_GROUP:(g + 1) * LANE_GROUP, :], nt,
                                preferred_element_type=F32) for g in range(NG)]

    def idx_keys(j, s_all):
        acc = jnp.zeros((tk, T), F32)
        for g in range(NG):
            for hh in range(G):
                acc = acc + w_rows[g * G + hh] * jnp.maximum(s_all[g][:, hh * T:(hh + 1) * T], 0.0)
        bits = pltpu.bitcast(acc, I32)
        key = jnp.where(bits >= 0, bits, bits ^ 0x7FFFFFFF)
        valid = (cmr + (t0 - j * tk)) >= 0
        key = jnp.where(valid, key, INT_MIN)
        keys_ref[j] = key.reshape(tk // 8, 8, T)
        u = key ^ INT_MIN
        for wg in range(WG):
            words = [u[(wg * 32 + r) * 8:(wg * 32 + r + 1) * 8, :] for r in range(32)]
            planes = _bit_transpose32(words)
            for q in range(32):
                planes_ref[j, wg, q] = planes[q]

    def p1_pair(jj, carry):
        s_a = idx_logits(2 * jj)
        s_b = idx_logits(2 * jj + 1)
        idx_keys(2 * jj, s_a)
        idx_keys(2 * jj + 1, s_b)
        return carry

    lax.fori_loop(0, lax.shift_right_logical(nk, 1), p1_pair, 0)

    @pl.when((nk & 1) == 1)
    def _():
        idx_keys(nk - 1, idx_logits(nk - 1))

    kf = float(k_top)
    nkt = keys_ref.shape[0]
    alive = [jnp.where(j < nk, jnp.full((8, T), -1, I32), jnp.zeros((8, T), I32))
             for j in range(nkt) for _ in range(WG)]
    cnt_above = jnp.zeros((1, T), F32)
    thr_u = jnp.zeros((1, T), I32)
    for q in range(32):
        ones = [alive[j * WG + wg] & planes_ref[j, wg, q] for j in range(nkt) for wg in range(WG)]
        c1 = jnp.sum(_tree_sum([lax.population_count(o) for o in ones]).astype(F32), axis=0, keepdims=True)
        ok = (cnt_above + c1) >= kf
        ok_b = jnp.broadcast_to(ok, (8, T))
        alive = [jnp.where(ok_b, o, a ^ o) for a, o in zip(alive, ones)]
        cnt_above = jnp.where(ok, cnt_above, cnt_above + c1)
        thr_u = thr_u | jnp.where(ok, jnp.int32(INT_MIN if q == 0 else 1 << (31 - q)), 0)
    n_tie = jnp.sum(_tree_sum([lax.population_count(a) for a in alive]).astype(F32), axis=0, keepdims=True)
    need = kf - cnt_above
    thr = jnp.maximum(thr_u ^ INT_MIN, INT_MIN + 1)

    surplus = jnp.where(thr_u != 0, n_tie - need, 0.0)
    has_ties = jnp.max(surplus) > 0.0

    @pl.when(has_ties)
    def _():
        thr_t = jnp.broadcast_to(thr, (tk, T))
        r2 = lax.broadcasted_iota(I32, (tk, tk), 0)
        c2 = lax.broadcasted_iota(I32, (tk, tk), 1)
        tri = jnp.where(r2 >= c2, 1.0, 0.0).astype(BF16)

        def fix_body(j, base):
            tile = keys_ref[j].reshape(tk, T)
            tie = tile == thr_t
            pref = jnp.dot(tri, jnp.where(tie, 1.0, 0.0).astype(BF16), preferred_element_type=F32) + base
            drop = jnp.logical_and(tie, pref > need)
            keys_ref[j] = jnp.where(drop, tile - 1, tile).reshape(tk // 8, 8, T)
            return pref[tk - 1:tk, :]

        lax.fori_loop(0, nk, fix_body, jnp.zeros((1, T), F32))

    m_ref[...] = jnp.full(m_ref.shape, NEG_BIG, F32)
    acc_ref[...] = jnp.zeros(acc_ref.shape, F32)
    thr_b = jnp.broadcast_to(thr, (tk, T))

    def tile_logits(j):
        off = pl.multiple_of(j * tk, tk)
        kx = kx_ref[pl.ds(off, tk), :]
        return [lax.dot_general(kx, qx_ref[g * LANE_GROUP:(g + 1) * LANE_GROUP, :], nt,
                                preferred_element_type=F32) for g in range(NG)]

    def tile_softmax(j, s_all):
        vtx = vtx_ref[j]
        sel = keys_ref[j].reshape(tk, T) >= thr_b
        sel_g = jnp.concatenate([sel] * G, axis=1) if G > 1 else sel
        for g in range(NG):
            s = jnp.where(sel_g, s_all[g], NEG_BIG)
            m_old = m_ref[g]
            m_new = jnp.maximum(m_old, jnp.max(s, axis=0, keepdims=True))
            p = jnp.exp2(s - m_new).astype(BF16)
            acc_ref[g] = jnp.exp2(m_old - m_new) * acc_ref[g] + jnp.dot(vtx, p, preferred_element_type=F32)
            m_ref[g] = m_new

    def p3_pair(jj, carry):
        s_a = tile_logits(2 * jj)
        s_b = tile_logits(2 * jj + 1)
        tile_softmax(2 * jj, s_a)
        tile_softmax(2 * jj + 1, s_b)
        return carry

    lax.fori_loop(0, lax.shift_right_logical(nk, 1), p3_pair, 0)

    @pl.when((nk & 1) == 1)
    def _():
        tile_softmax(nk - 1, tile_logits(nk - 1))

    def head_out(h):
        g, hh = h // G, h % G
        a = acc_ref[g]
        return a[0:A_HEAD_DIM, hh * T:(hh + 1) * T] / a[A_HEAD_DIM:A_HEAD_DIM + 1, hh * T:(hh + 1) * T]

    for h in range(0, A_HEADS, 2):
        pair = jnp.concatenate([head_out(h), head_out(h + 1)], axis=0)
        o_ref[:, h * A_HEAD_DIM:(h + 2) * A_HEAD_DIM] = jnp.transpose(pair).astype(BF16)


def _alibi_tables(seq, T):
    log2e = 1.4426950408889634
    pos = jnp.arange(seq, dtype=I32)
    hi = (pos // 64).astype(F32)
    lo = (pos % 64).astype(F32)
    z = jnp.zeros((seq,), F32)
    pos_cols = [z] * 64 + [hi, hi, hi, lo, lo, lo] + [z] * 58
    postab = jnp.stack(pos_cols, axis=1).astype(BF16)
    rows = []
    for h in range(A_HEADS):
        c = jnp.asarray(2.0 ** (-8.0 * (h + 1) / A_HEADS) * log2e, F32)
        c1 = c.astype(BF16)
        c2 = (c - c1.astype(F32)).astype(BF16)
        c3 = (c - c1.astype(F32) - c2.astype(F32)).astype(BF16)
        r = jnp.zeros((64,), BF16)
        r = r.at[0].set(c1 * 64).at[1].set(c2 * 64).at[2].set(c3 * 64).at[3].set(c1).at[4].set(c2).at[5].set(c3)
        rows.append(jnp.broadcast_to(r[None, :], (T, 64)))
    qext = jnp.concatenate(rows, axis=0)
    return postab, qext


def _dsa(p1, p3, vt, bsz, seq, T, tk):
    m = p1.shape[0]
    nq = seq // T
    nkt = seq // tk
    k_top = min(TOPK_MAX, seq // 4)
    postab, qext = _alibi_tables(seq, T)
    kern = functools.partial(_dsa_kernel, T=T, tk=tk, k_top=k_top)
    ng = A_HEADS * T // LANE_GROUP
    return pl.pallas_call(
        kern,
        grid=(bsz, nq),
        in_specs=[
            pl.BlockSpec((T, 1024), lambda b, i: (b * nq + i, 0)),
            pl.BlockSpec((seq, 256), lambda b, i: (b, 4)),
            pl.BlockSpec((nkt, A_HEAD_DIM, tk), lambda b, i: (b, 0, 0)),
            pl.BlockSpec((T, 128), lambda b, i: (b * nq + i, 4)),
            pl.BlockSpec((A_HEADS * T, 64), lambda b, i: (0, 0)),
            pl.BlockSpec((seq, 128), lambda b, i: (0, 0)),
        ],
        out_specs=pl.BlockSpec((T, 512), lambda b, i: (b * nq + i, 0)),
        out_shape=jax.ShapeDtypeStruct((m, 512), BF16),
        scratch_shapes=[
            pltpu.VMEM((seq, 128), BF16),
            pltpu.VMEM((nkt, A_HEAD_DIM + ONES_ROWS, tk), BF16),
            pltpu.VMEM((A_HEADS * T, 128), BF16),
            pltpu.VMEM((A_HEADS * T, IDX_DIM), BF16),
            pltpu.VMEM((nkt, tk // 8, 8, T), I32),
            pltpu.VMEM((nkt, tk // 256, 32, 8, T), I32),
            pltpu.VMEM((ng, 1, LANE_GROUP), F32),
            pltpu.VMEM((ng, A_HEAD_DIM + ONES_ROWS, LANE_GROUP), F32),
        ],
        compiler_params=_cparams(("arbitrary", "arbitrary")),
    )(p1, p1, vt, p3, qext, postab)


def _gla_kernel(p2_ref, p3_ref, wa2_ref, ba_ref, g_ref, o_ref, st_ref, *, R):
    C = GLA_CHUNK

    @pl.when(pl.program_id(1) == 0)
    def _():
        st_ref[...] = jnp.zeros(st_ref.shape, F32)

    ba = p3_ref[:, 512 + BA_OFF:512 + BA_OFF + GATE_RANK]
    xg = jnp.dot(ba, wa2_ref[...], precision=HIGHEST, preferred_element_type=F32) + ba_ref[...]
    log_a = (jnp.minimum(xg, 0.0) - jnp.log(1.0 + jnp.exp(-jnp.abs(xg)))) * (1.0 / GATE_TAU)

    r_i = lax.broadcasted_iota(I32, (C, C), 0)
    c_i = lax.broadcasted_iota(I32, (C, C), 1)
    tri_b = r_i >= c_i
    tri_f = tri_b.astype(F32)
    nt = (((1,), (1,)), ((), ()))
    tn = (((0,), (0,)), ((), ()))
    gamma = g_ref[...]

    sts = [st_ref[h] for h in range(B_HEADS)]
    heads = range(B_HEADS)
    kss = [slice(h * B_KEY_DIM, (h + 1) * B_KEY_DIM) for h in heads]
    bs = [jnp.dot(tri_f, log_a[c * C:(c + 1) * C, :], precision=HIGHEST, preferred_element_type=F32)
          for c in range(R // C)]
    for c in range(R // C):
        rows = slice(c * C, (c + 1) * C)
        b = bs[c]
        b_last = b[C - 1:C, :]
        b_mid = b[C // 2 - 1:C // 2, :]
        q = p2_ref[rows, 0:256].astype(F32) * (B_KEY_DIM ** -0.5)
        k = p2_ref[rows, 256:512].astype(F32)
        qe = (q * jnp.exp(b)).astype(BF16)
        qm = (q * jnp.exp(b - b_mid)).astype(BF16)
        km = (k * jnp.exp(b_mid - b)).astype(BF16)
        kl = (k * jnp.exp(b_last - b)).astype(BF16)
        dec = jnp.exp(b_last)
        vs_ = [p2_ref[rows, 512 + h * B_VAL_DIM:512 + (h + 1) * B_VAL_DIM] for h in heads]
        a_s = [lax.dot_general(qm[:, kss[h]], km[:, kss[h]], nt, preferred_element_type=F32) for h in heads]
        o_s = [lax.dot_general(qe[:, kss[h]], sts[h].astype(BF16), nt, preferred_element_type=F32) for h in heads]
        x_s = [lax.dot_general(vs_[h], kl[:, kss[h]], tn, preferred_element_type=F32) for h in heads]
        a_s = [jnp.where(tri_b, a, 0.0).astype(BF16) for a in a_s]
        o_s = [o_s[h] + jnp.dot(a_s[h], vs_[h], preferred_element_type=F32) for h in heads]
        sts = [sts[h] * dec[:, kss[h]] + x_s[h] for h in heads]
        for h in heads:
            o = o_s[h]
            ms = jnp.mean(o * o, axis=1, keepdims=True)
            on = o * lax.rsqrt(ms + NORM_EPS) * gamma
            gate = p3_ref[rows, h * B_VAL_DIM:(h + 1) * B_VAL_DIM]
            o_ref[rows, h * B_VAL_DIM:(h + 1) * B_VAL_DIM] = (on * (gate * _sigmoid(gate))).astype(BF16)
    for h in range(B_HEADS):
        st_ref[h] = sts[h]


def _gla(p2, p3, w_a2, b_a, gamma, bsz, seq, R):
    m = p2.shape[0]
    nr = seq // R
    kern = functools.partial(_gla_kernel, R=R)
    return pl.pallas_call(
        kern,
        grid=(bsz, nr),
        in_specs=[
            pl.BlockSpec((R, N2), lambda b, i: (b * nr + i, 0)),
            pl.BlockSpec((R, N3), lambda b, i: (b * nr + i, 0)),
            pl.BlockSpec((GATE_RANK, 256), lambda b, i: (0, 0)),
            pl.BlockSpec((1, 256), lambda b, i: (0, 0)),
            pl.BlockSpec((1, B_VAL_DIM), lambda b, i: (0, 0)),
        ],
        out_specs=pl.BlockSpec((R, 512), lambda b, i: (b * nr + i, 0)),
        out_shape=jax.ShapeDtypeStruct((m, 512), BF16),
        scratch_shapes=[pltpu.VMEM((B_HEADS, B_VAL_DIM, B_KEY_DIM), F32)],
        compiler_params=_cparams(("arbitrary", "arbitrary")),
    )(p2, p3, w_a2, b_a.reshape(1, 256), gamma.reshape(1, B_VAL_DIM))


def _resid_ln(x, y, gate, ln_g, ln_b, alpha):
    z = alpha * x + (1.0 + gate) * y
    mu = jnp.mean(z, axis=1, keepdims=True)
    zc = z - mu
    var = jnp.mean(zc * zc, axis=1, keepdims=True)
    return zc * lax.rsqrt(var + NORM_EPS) * ln_g + ln_b


def _outproj_kernel(ya_ref, ob_ref, x_ref, w_ref, mod_ref, lng_ref, lnb_ref, *rest, alpha, routed):
    if routed:
        wr_ref, xo_ref, h_ref, lg_ref = rest
    else:
        xo_ref, h_ref = rest
    y = jnp.dot(ya_ref[...], w_ref[0:512, :], preferred_element_type=F32)
    y = y + jnp.dot(ob_ref[...], w_ref[512:1024, :], preferred_element_type=F32)
    xn = _resid_ln(x_ref[...], y, mod_ref[0, 2:3, :], lng_ref[...], lnb_ref[...], alpha)
    xo_ref[...] = xn
    h = xn * (1.0 + mod_ref[0, 4:5, :]) + mod_ref[0, 3:4, :]
    if routed:
        h_ref[...] = h
        lane = lax.broadcasted_iota(I32, (1, 128), 1)
        lg = jnp.zeros((h.shape[0], 128), F32)
        for e in range(N_EXPERTS):
            col = jnp.sum(h * wr_ref[e:e + 1, :], axis=1, keepdims=True)
            lg = lg + col * jnp.where(lane == e, 1.0, 0.0)
        lg_ref[...] = lg
    else:
        h_ref[...] = h.astype(BF16)


def _outproj(ya, ob, x2d, w_out, mod_l, ln_g, ln_b, w_router_p, seq, tm, alpha):
    m, d = x2d.shape
    tpb = seq // tm
    routed = w_router_p is not None
    kern = functools.partial(_outproj_kernel, alpha=alpha, routed=routed)
    in_specs = [
        pl.BlockSpec((tm, 512), lambda i: (i, 0)),
        pl.BlockSpec((tm, 512), lambda i: (i, 0)),
        pl.BlockSpec((tm, d), lambda i: (i, 0)),
        pl.BlockSpec((d, d), lambda i: (0, 0)),
        pl.BlockSpec((1, 6, d), lambda i: (i // tpb, 0, 0)),
        pl.BlockSpec((1, d), lambda i: (0, 0)),
        pl.BlockSpec((1, d), lambda i: (0, 0)),
    ]
    args = [ya, ob, x2d, w_out, mod_l, ln_g.reshape(1, d), ln_b.reshape(1, d)]
    out_specs = [pl.BlockSpec((tm, d), lambda i: (i, 0)), pl.BlockSpec((tm, d), lambda i: (i, 0))]
    out_shape = [jax.ShapeDtypeStruct((m, d), F32), jax.ShapeDtypeStruct((m, d), F32 if routed else BF16)]
    if routed:
        in_specs.append(pl.BlockSpec((N_EXPERTS, d), lambda i: (0, 0)))
        args.append(w_router_p)
        out_specs.append(pl.BlockSpec((tm, 128), lambda i: (i, 0)))
        out_shape.append(jax.ShapeDtypeStruct((m, 128), F32))
    return pl.pallas_call(
        kern,
        grid=(m // tm,),
        in_specs=in_specs,
        out_specs=out_specs,
        out_shape=out_shape,
        compiler_params=_cparams(("arbitrary",)),
    )(*args)


def _swiglu_acc(h, wg_ref, wu_ref, wd_ref, acc_ref, tf, widx):
    c0 = 0
    while c0 < tf:
        c1 = min(c0 + 512, tf)
        if widx is None:
            wg, wu, wd = wg_ref[:, c0:c1], wu_ref[:, c0:c1], wd_ref[c0:c1, :]
        else:
            wg, wu, wd = wg_ref[0, :, c0:c1], wu_ref[0, :, c0:c1], wd_ref[0, c0:c1, :]
        g = jnp.dot(h, wg, preferred_element_type=F32)
        u = jnp.dot(h, wu, preferred_element_type=F32)
        a = (g * _sigmoid(g) * u).astype(BF16)
        acc_ref[...] += jnp.dot(a, wd, preferred_element_type=F32)
        c0 = c1


def _ffn_dense_kernel(h_ref, x_ref, wg_ref, wu_ref, wd_ref, mod_ref, lng_ref, lnb_ref, xo_ref, acc_ref,
                      *, tf, alpha):
    f = pl.program_id(1)

    @pl.when(f == 0)
    def _():
        acc_ref[...] = jnp.zeros(acc_ref.shape, F32)

    _swiglu_acc(h_ref[...], wg_ref, wu_ref, wd_ref, acc_ref, tf, None)

    @pl.when(f == pl.num_programs(1) - 1)
    def _():
        xo_ref[...] = _resid_ln(x_ref[...], acc_ref[...], mod_ref[0, 5:6, :], lng_ref[...], lnb_ref[...], alpha)


def _ffn_dense(h, x2d, wg, wu, wd, mod_l, ln_g, ln_b, seq, tm, tf, alpha):
    m, d = x2d.shape
    dff = wg.shape[1]
    tpb = seq // tm
    kern = functools.partial(_ffn_dense_kernel, tf=tf, alpha=alpha)
    return pl.pallas_call(
        kern,
        grid=(m // tm, dff // tf),
        in_specs=[
            pl.BlockSpec((tm, d), lambda i, f: (i, 0)),
            pl.BlockSpec((tm, d), lambda i, f: (i, 0)),
            pl.BlockSpec((d, tf), lambda i, f: (0, f)),
            pl.BlockSpec((d, tf), lambda i, f: (0, f)),
            pl.BlockSpec((tf, d), lambda i, f: (f, 0)),
            pl.BlockSpec((1, 6, d), lambda i, f: (i // tpb, 0, 0)),
            pl.BlockSpec((1, d), lambda i, f: (0, 0)),
            pl.BlockSpec((1, d), lambda i, f: (0, 0)),
        ],
        out_specs=pl.BlockSpec((tm, d), lambda i, f: (i, 0)),
        out_shape=jax.ShapeDtypeStruct((m, d), F32),
        scratch_shapes=[pltpu.VMEM((tm, d), F32)],
        compiler_params=_cparams(("arbitrary", "arbitrary")),
    )(h, x2d, wg, wu, wd, mod_l, ln_g.reshape(1, d), ln_b.reshape(1, d))


def _ffn_grouped_kernel(te_ref, nu_ref, xs_ref, wg_ref, wu_ref, wd_ref, yo_ref, acc_ref, *, tf):
    i = pl.program_id(0)
    f = pl.program_id(1)
    used = i < nu_ref[0]

    @pl.when(jnp.logical_and(used, f == 0))
    def _():
        acc_ref[...] = jnp.zeros(acc_ref.shape, F32)

    @pl.when(used)
    def _():
        _swiglu_acc(xs_ref[...].astype(BF16), wg_ref, wu_ref, wd_ref, acc_ref, tf, 0)

    @pl.when(jnp.logical_and(used, f == pl.num_programs(1) - 1))
    def _():
        yo_ref[...] = acc_ref[...]


def _ffn_grouped(xs, tile_expert, n_used, wg, wu, wd, tm, tf):
    n_rows, d = xs.shape
    dff = wg.shape[2]
    nf = dff // tf

    def w_map(i, f, te, nu):
        return (te[i], 0, jnp.where(i < nu[0], f, nf - 1))

    def wd_map(i, f, te, nu):
        return (te[i], jnp.where(i < nu[0], f, nf - 1), 0)

    kern = functools.partial(_ffn_grouped_kernel, tf=tf)
    return pl.pallas_call(
        kern,
        grid_spec=pltpu.PrefetchScalarGridSpec(
            num_scalar_prefetch=2,
            grid=(n_rows // tm, nf),
            in_specs=[
                pl.BlockSpec((tm, d), lambda i, f, te, nu: (i, 0)),
                pl.BlockSpec((1, d, tf), w_map),
                pl.BlockSpec((1, d, tf), w_map),
                pl.BlockSpec((1, tf, d), wd_map),
            ],
            out_specs=pl.BlockSpec((tm, d), lambda i, f, te, nu: (i, 0)),
            scratch_shapes=[pltpu.VMEM((tm, d), F32)],
        ),
        out_shape=jax.ShapeDtypeStruct((n_rows, d), F32),
        compiler_params=_cparams(("arbitrary", "arbitrary")),
    )(tile_expert, n_used, xs, wg, wu, wd)


def _row_copy(src, dst, sem):
    return pltpu.make_async_copy(src, dst, sem)


ROW_UNROLL = 16


def _scatter_kernel(dest_ref, h_ref, xs_ref, sem, *, ts):
    def start(rb, carry):
        for u in range(ROW_UNROLL):
            r = rb * ROW_UNROLL + u
            for s in range(2):
                _row_copy(h_ref.at[pl.ds(r, 1), :], xs_ref.at[pl.ds(dest_ref[0, 0, 2 * r + s], 1), :],
                          sem).start(priority=s)
        return carry

    lax.fori_loop(0, ts // ROW_UNROLL, start, 0)
    for s in range(2):
        _row_copy(h_ref, xs_ref.at[pl.ds(0, ts), :], sem).wait()


def _scatter_rows(h, dest, n_rows, ts):
    n_tok, d = h.shape
    kern = functools.partial(_scatter_kernel, ts=ts)
    return pl.pallas_call(
        kern,
        grid=(n_tok // ts,),
        in_specs=[
            pl.BlockSpec((1, 1, 2 * ts), lambda i: (i, 0, 0), memory_space=pltpu.SMEM),
            pl.BlockSpec((ts, d), lambda i: (i, 0)),
        ],
        out_specs=pl.BlockSpec(memory_space=pl.ANY),
        out_shape=jax.ShapeDtypeStruct((n_rows, d), F32),
        scratch_shapes=[pltpu.SemaphoreType.DMA(())],
        compiler_params=_cparams(("arbitrary",)),
    )(dest.reshape(n_tok // ts, 1, 2 * ts), h)


def _combine_kernel(dest_ref, gate_ref, x_ref, mod_ref, lng_ref, lnb_ref, yb_ref, xo_ref, r0_ref, r1_ref, sem,
                    *, ts, alpha):
    bufs = (r0_ref, r1_ref)

    def start(rb, carry):
        for u in range(ROW_UNROLL):
            r = rb * ROW_UNROLL + u
            for s in range(2):
                _row_copy(yb_ref.at[pl.ds(dest_ref[0, 0, 2 * r + s], 1), :], bufs[s].at[pl.ds(r, 1), :],
                          sem).start(priority=s)
        return carry

    lax.fori_loop(0, ts // ROW_UNROLL, start, 0)
    for s in range(2):
        _row_copy(yb_ref.at[pl.ds(0, ts), :], bufs[s], sem).wait()

    y = gate_ref[:, 0:1] * r0_ref[...] + gate_ref[:, 1:2] * r1_ref[...]
    xo_ref[...] = _resid_ln(x_ref[...], y, mod_ref[0, 5:6, :], lng_ref[...], lnb_ref[...], alpha)


def _combine(dest, gates, x2d, mod_l, ln_g, ln_b, yb, seq, ts, alpha):
    n_tok, d = x2d.shape
    tpb = seq // ts
    kern = functools.partial(_combine_kernel, ts=ts, alpha=alpha)
    return pl.pallas_call(
        kern,
        grid=(n_tok // ts,),
        in_specs=[
            pl.BlockSpec((1, 1, 2 * ts), lambda i: (i, 0, 0), memory_space=pltpu.SMEM),
            pl.BlockSpec((ts, 2), lambda i: (i, 0)),
            pl.BlockSpec((ts, d), lambda i: (i, 0)),
            pl.BlockSpec((1, 6, d), lambda i: (i // tpb, 0, 0)),
            pl.BlockSpec((1, d), lambda i: (0, 0)),
            pl.BlockSpec((1, d), lambda i: (0, 0)),
            pl.BlockSpec(memory_space=pl.ANY),
        ],
        out_specs=pl.BlockSpec((ts, d), lambda i: (i, 0)),
        out_shape=jax.ShapeDtypeStruct((n_tok, d), F32),
        scratch_shapes=[pltpu.VMEM((ts, d), F32), pltpu.VMEM((ts, d), F32), pltpu.SemaphoreType.DMA(())],
        compiler_params=_cparams(("arbitrary",)),
    )(dest.reshape(n_tok // ts, 1, 2 * ts), gates, x2d, mod_l, ln_g.reshape(1, d), ln_b.reshape(1, d), yb)


def _route(logits, tm_e):
    n_tok = logits.shape[0]
    top_val, top_idx = lax.top_k(logits, 2)
    gates = jax.nn.softmax(top_val, axis=-1)
    e_flat = top_idx.reshape(-1).astype(I32)
    onehot = (e_flat[:, None] == jnp.arange(N_EXPERTS, dtype=I32)[None, :]).astype(I32)
    csum = jnp.cumsum(onehot, axis=0)
    counts = csum[-1]
    rank = jnp.sum(csum * onehot, axis=1) - 1
    padded = ((counts + tm_e - 1) // tm_e) * tm_e
    pend = jnp.cumsum(padded)
    pstart = pend - padded
    dest = (pstart[e_flat] + rank).astype(I32)
    n_tiles = (2 * n_tok) // tm_e + N_EXPERTS
    n_used = (pend[-1] // tm_e).astype(I32)
    tile_start = jnp.arange(n_tiles, dtype=I32) * tm_e
    tile_expert = jnp.minimum(jnp.searchsorted(pend, tile_start, side='right'), N_EXPERTS - 1).astype(I32)
    last_e = tile_expert[jnp.maximum(n_used - 1, 0)]
    tile_expert = jnp.where(jnp.arange(n_tiles) < n_used, tile_expert, last_e)
    return dest, gates, tile_expert, n_used.reshape(1), n_tiles


def _reorder_w_in(w_in_l):
    widths = (512, 64, 64, 512, 64, 8, 256, 256, 512, 512, 16)
    offs = [0]
    for wd_ in widths:
        offs.append(offs[-1] + wd_)
    seg = [w_in_l[:, offs[k]:offs[k + 1]] for k in range(len(widths))]
    aq, ak, av, iq, ik, iw, bq, bk, bv, bg, ba = seg
    d = w_in_l.shape[0]
    z = lambda n: jnp.zeros((d, n), w_in_l.dtype)
    aq = aq * (1.4426950408889634 * A_HEAD_DIM ** -0.5)
    w_r = jnp.concatenate([aq, iq, ak, z(64), ik, z(64), bq, bk, bv, bg, iw, ba, z(104)], axis=1).astype(BF16)
    return w_r, jnp.transpose(av).astype(BF16)


def kernel(x, c, w_in, w_a2, b_a, gla_norm_g, w_out, w_mod, b_mod, ln_g, ln_b, ffn_w_gate, ffn_w_up, ffn_w_down,
           w_router, b_router, moe_w_gate, moe_w_up, moe_w_down):
    bsz, seq, d = x.shape
    depth = w_in.shape[0]
    dff = ffn_w_gate.shape[2]
    m = bsz * seq
    alpha = (2 * depth) ** 0.25

    tm_p = min(512, seq)
    tm_f = min(1024, seq)
    tm_e = min(1024, seq)
    T = min(256, seq)
    tk = min(512, seq)
    R = min(512, seq)
    ts = min(512, seq)
    tf = dff // 2 if (dff // 2) % 128 == 0 else dff

    mod = _modulation(c, w_mod, b_mod).reshape(depth, bsz, 6, d)
    x2d = x.reshape(m, d)
    for l in range(depth):
        mod_l = mod[l]
        w_r, w_vt = _reorder_w_in(w_in[l])
        p1, p2, p3, vt = _proj(x2d, mod_l, w_r, w_vt, seq, tm_p, tk)
        ya = _dsa(p1, p3, vt, bsz, seq, T, tk)
        ob = _gla(p2, p3, w_a2[l], b_a[l], gla_norm_g[l], bsz, seq, R)
        j = l // 2
        if l % 2 == 0:
            x1, h2 = _outproj(ya, ob, x2d, w_out[l].astype(BF16), mod_l, ln_g[l, 0], ln_b[l, 0], None,
                              seq, tm_p, alpha)
            x2d = _ffn_dense(h2, x1, ffn_w_gate[j].astype(BF16), ffn_w_up[j].astype(BF16),
                             ffn_w_down[j].astype(BF16), mod_l, ln_g[l, 1], ln_b[l, 1], seq, tm_f, tf, alpha)
        else:
            wr = jnp.transpose(w_router[j])
            x1, h2, lg = _outproj(ya, ob, x2d, w_out[l].astype(BF16), mod_l, ln_g[l, 0], ln_b[l, 0], wr,
                                  seq, tm_p, alpha)
            logits = lg[:, :N_EXPERTS] + b_router[j][None, :]
            dest, gates, tile_expert, n_used, n_tiles = _route(logits, tm_e)
            xs = _scatter_rows(h2, dest, n_tiles * tm_e, ts)
            yb = _ffn_grouped(xs, tile_expert, n_used, moe_w_gate[j].astype(BF16), moe_w_up[j].astype(BF16),
                              moe_w_down[j].astype(BF16), tm_e, tf)
            x2d = _combine(dest, gates, x1, mod_l, ln_g[l, 1], ln_b[l, 1], yb, seq, ts, alpha)
    return x2d.reshape(bsz, seq, d)
```
